```python
import math
import jax, jax.numpy as jnp
from jax import lax
import numpy as np

D_MODEL = 1024
BATCH = 4
SEQ = 4096
DEPTH = 2
DEC_BATCH = 32
DEC_SEQ = 1
PAST_LEN = 8192
PAGE_SIZE = 128

HEAD_DIM = 64
SELF_WIDTH = 3 * D_MODEL // 4
N_HEADS_FOX = SELF_WIDTH // HEAD_DIM
N_HEADS_DIFF = SELF_WIDTH // (2 * HEAD_DIM)
N_MEM = 256
N_HEADS_MEM = 4
MEM_WIDTH = D_MODEL - SELF_WIDTH
MEM_HEAD_DIM = MEM_WIDTH // N_HEADS_MEM
D_FF = -(-8 * D_MODEL // (3 * 256)) * 256
ROPE_THETA = 10000.0
Q_BLOCK = 128
NORM_EPS = 1e-6
NEG_INF = -1e30
N_FOX = (DEPTH + 1) // 2
N_DIFF = DEPTH // 2
FOX_IN = 3 * SELF_WIDTH + N_HEADS_FOX + MEM_WIDTH
DIFF_IN = 3 * SELF_WIDTH + MEM_WIDTH
POOL_NUM, POOL_DEN = 5, 4
FORGET_BIAS_INIT = 2.0

kernel_name = 'fox_diff_memory_hybrid_step'


def rms_norm(x, g):
    xf = x.astype(jnp.float32)
    y = xf * lax.rsqrt(jnp.mean(xf * xf, axis=-1, keepdims=True) + NORM_EPS)
    return (y * g.astype(jnp.float32)).astype(x.dtype)


def rope(x, pos):
    d = x.shape[-1]
    inv = ROPE_THETA ** (-jnp.arange(0, d, 2, dtype=jnp.float32) / d)
    ang = pos.astype(jnp.float32)[:, None] * inv[None, :]
    cos = jnp.cos(ang)[:, None, :]
    sin = jnp.sin(ang)[:, None, :]
    xf = x.astype(jnp.float32)
    x1, x2 = xf[..., : d // 2], xf[..., d // 2:]
    return jnp.concatenate([x1 * cos - x2 * sin, x1 * sin + x2 * cos], axis=-1).astype(x.dtype)


def sweep_query_blocks(block_fn, seq_len):
    starts = jnp.arange(seq_len // Q_BLOCK) * Q_BLOCK
    out = lax.map(block_fn, starts)
    nb, b, qb, h, dv = out.shape
    return jnp.moveaxis(out, 0, 1).reshape(b, nb * qb, h, dv)


def causal_mask(q_pos, k_pos):
    return k_pos[None, :] <= q_pos[:, None]


def masked_softmax(s, mask):
    return jax.nn.softmax(jnp.where(mask, s, NEG_INF), axis=-1)


def apply_weights(p, v):
    return jnp.einsum('bhqk,bkhd->bqhd', p.astype(v.dtype), v)


def gather_pages(cache, j, page_table):
    g = cache[j, page_table]
    return g.reshape(g.shape[0], -1, *g.shape[3:])


def fox_project(xn, w_in, b_f):
    b, t, _ = xn.shape
    w = SELF_WIDTH
    z = xn @ w_in
    q = z[..., :w].reshape(b, t, N_HEADS_FOX, HEAD_DIM)
    k = z[..., w:2 * w].reshape(b, t, N_HEADS_FOX, HEAD_DIM)
    v = z[..., 2 * w:3 * w].reshape(b, t, N_HEADS_FOX, HEAD_DIM)
    logf = jax.nn.log_sigmoid((z[..., 3 * w:3 * w + N_HEADS_FOX] + b_f).astype(jnp.float32))
    qm = z[..., 3 * w + N_HEADS_FOX:].reshape(b, t, N_HEADS_MEM, MEM_HEAD_DIM)
    return q, k, v, logf, qm


def fox_weights(qb, k, c_q, c_k, mask):
    s = jnp.einsum('bqhd,bkhd->bhqk', qb, k).astype(jnp.float32) * (HEAD_DIM ** -0.5)
    s = s + c_q[..., :, None] - c_k[..., None, :]
    return masked_softmax(s, mask)


def fox_prompt(q, k, v, logf):
    seq_len = q.shape[1]
    c = jnp.swapaxes(jnp.cumsum(logf, axis=1), 1, 2)
    k_pos = jnp.arange(seq_len)

    def block(start):
        qb = lax.dynamic_slice_in_dim(q, start, Q_BLOCK, axis=1)
        c_q = lax.dynamic_slice_in_dim(c, start, Q_BLOCK, axis=2)
        mask = causal_mask(start + jnp.arange(Q_BLOCK), k_pos)
        return apply_weights(fox_weights(qb, k, c_q, c, mask), v)

    return sweep_query_blocks(block, seq_len)


def fox_sample(q, k, v, logf, k_past, v_past, logf_past):
    past_len, t = k_past.shape[1], q.shape[1]
    k_all = jnp.concatenate([k_past, k], axis=1)
    v_all = jnp.concatenate([v_past, v], axis=1)
    lf_all = jnp.concatenate([logf_past.astype(jnp.float32), logf], axis=1)
    c = jnp.swapaxes(jnp.cumsum(lf_all, axis=1), 1, 2)
    mask = causal_mask(past_len + jnp.arange(t), jnp.arange(past_len + t))
    return apply_weights(fox_weights(q, k_all, c[:, :, past_len:], c, mask), v_all)


def diff_project(xn, w_in, pos):
    b, t, _ = xn.shape
    w = SELF_WIDTH
    z = xn @ w_in
    q = rope(z[..., :w].reshape(b, t, 2 * N_HEADS_DIFF, HEAD_DIM), pos)
    k = rope(z[..., w:2 * w].reshape(b, t, 2 * N_HEADS_DIFF, HEAD_DIM), pos)
    q = q.reshape(b, t, N_HEADS_DIFF, 2 * HEAD_DIM)
    k = k.reshape(b, t, N_HEADS_DIFF, 2 * HEAD_DIM)
    v = z[..., 2 * w:3 * w].reshape(b, t, N_HEADS_DIFF, 2 * HEAD_DIM)
    qm = z[..., 3 * w:].reshape(b, t, N_HEADS_MEM, MEM_HEAD_DIM)
    return q, k, v, qm


def diff_lambda(lq1, lk1, lq2, lk2, lam_init):
    f32 = jnp.float32
    return (jnp.exp(jnp.sum(lq1.astype(f32) * lk1.astype(f32)))
            - jnp.exp(jnp.sum(lq2.astype(f32) * lk2.astype(f32))) + lam_init)


def diff_weights(qb, k, lam, mask):
    scale = HEAD_DIM ** -0.5
    s1 = jnp.einsum('bqhd,bkhd->bhqk', qb[..., :HEAD_DIM], k[..., :HEAD_DIM]).astype(jnp.float32) * scale
    s2 = jnp.einsum('bqhd,bkhd->bhqk', qb[..., HEAD_DIM:], k[..., HEAD_DIM:]).astype(jnp.float32) * scale
    return masked_softmax(s1, mask) - lam * masked_softmax(s2, mask)


def diff_prompt(q, k, v, lam):
    seq_len = q.shape[1]
    k_pos = jnp.arange(seq_len)

    def block(start):
        qb = lax.dynamic_slice_in_dim(q, start, Q_BLOCK, axis=1)
        mask = causal_mask(start + jnp.arange(Q_BLOCK), k_pos)
        return apply_weights(diff_weights(qb, k, lam, mask), v)

    return sweep_query_blocks(block, seq_len)


def diff_sample(q, k, v, k_past, v_past, lam):
    past_len, t = k_past.shape[1], q.shape[1]
    k_all = jnp.concatenate([k_past, k], axis=1)
    v_all = jnp.concatenate([v_past, v], axis=1)
    mask = causal_mask(past_len + jnp.arange(t), jnp.arange(past_len + t))
    return apply_weights(diff_weights(q, k_all, lam, mask), v_all)


def diff_head_norm(o, g, lam_init):
    return rms_norm(o, g) * (1.0 - lam_init)


def memory_kv(mem, g, w):
    b, n, _ = mem.shape
    kv = rms_norm(mem, g) @ w
    mk = kv[..., :MEM_WIDTH].reshape(b, n, N_HEADS_MEM, MEM_HEAD_DIM)
    mv = kv[..., MEM_WIDTH:].reshape(b, n, N_HEADS_MEM, MEM_HEAD_DIM)
    return mk, mv


def memory_attend(qm, mk, mv):
    s = jnp.einsum('bthd,bmhd->bhtm', qm, mk).astype(jnp.float32) * (MEM_HEAD_DIM ** -0.5)
    p = jax.nn.softmax(s, axis=-1)
    return jnp.einsum('bhtm,bmhd->bthd', p.astype(mv.dtype), mv)


def finish_layer(h, o_self, qm, mk, mv, g_post_mix, w_out, g_pre_ffn, g_post_ffn, w_gate_up, w_down):
    b, t, _ = h.shape
    o_mem = memory_attend(qm, mk, mv)
    merged = jnp.concatenate([o_self.reshape(b, t, SELF_WIDTH), o_mem.reshape(b, t, MEM_WIDTH)], axis=-1)
    h = h + rms_norm(merged @ w_out, g_post_mix)
    u = rms_norm(h, g_pre_ffn) @ w_gate_up
    f = (jax.nn.silu(u[..., :D_FF]) * u[..., D_FF:]) @ w_down
    return h + rms_norm(f, g_post_ffn)


def setup_inputs(seed: int = 0) -> dict:
    key = jax.random.key(seed)
    ks = jax.random.split(key, 28)
    f32 = jnp.float32

    def nrm(i, shape, scale):
        return jax.random.normal(ks[i], shape, f32) * scale

    n_pages = PAST_LEN // PAGE_SIZE
    n_pool = (DEC_BATCH * n_pages * POOL_NUM) // POOL_DEN
    page_table = jax.random.permutation(ks[9], n_pool)[: DEC_BATCH * n_pages]
    page_table = page_table.reshape(DEC_BATCH, n_pages).astype(jnp.int32)
    return {
        'x_prompt': nrm(0, (BATCH, SEQ, D_MODEL), 1.0),
        'x_sample': nrm(1, (DEC_BATCH, DEC_SEQ, D_MODEL), 1.0),
        'cache_fox_k': nrm(2, (N_FOX, n_pool, PAGE_SIZE, N_HEADS_FOX, HEAD_DIM), 1.0),
        'cache_fox_v': nrm(3, (N_FOX, n_pool, PAGE_SIZE, N_HEADS_FOX, HEAD_DIM), 1.0),
        'cache_fox_logf': jax.nn.log_sigmoid(nrm(4, (N_FOX, n_pool, PAGE_SIZE, N_HEADS_FOX), 1.0) + FORGET_BIAS_INIT),
        'cache_diff_k': nrm(5, (N_DIFF, n_pool, PAGE_SIZE, N_HEADS_DIFF, 2 * HEAD_DIM), 1.0),
        'cache_diff_v': nrm(6, (N_DIFF, n_pool, PAGE_SIZE, N_HEADS_DIFF, 2 * HEAD_DIM), 1.0),
        'cache_mem_k': nrm(7, (DEPTH, DEC_BATCH, N_MEM, N_HEADS_MEM, MEM_HEAD_DIM), 1.0),
        'cache_mem_v': nrm(8, (DEPTH, DEC_BATCH, N_MEM, N_HEADS_MEM, MEM_HEAD_DIM), 1.0),
        'page_table': page_table,
        'mem_prompt': nrm(10, (BATCH, N_MEM, D_MODEL), 1.0),
        'w_in_fox': nrm(11, (N_FOX, D_MODEL, FOX_IN), D_MODEL ** -0.5),
        'b_f_fox': FORGET_BIAS_INIT + nrm(12, (N_FOX, N_HEADS_FOX), 0.1),
        'w_in_diff': nrm(13, (N_DIFF, D_MODEL, DIFF_IN), D_MODEL ** -0.5),
        'lam_q1': nrm(14, (N_DIFF, HEAD_DIM), 0.1),
        'lam_k1': nrm(15, (N_DIFF, HEAD_DIM), 0.1),
        'lam_q2': nrm(16, (N_DIFF, HEAD_DIM), 0.1),
        'lam_k2': nrm(17, (N_DIFF, HEAD_DIM), 0.1),
        'g_subln': 1.0 + nrm(18, (N_DIFF, 2 * HEAD_DIM), 0.02),
        'g_pre_mix': 1.0 + nrm(19, (DEPTH, D_MODEL), 0.02),
        'g_post_mix': 1.0 + nrm(20, (DEPTH, D_MODEL), 0.02),
        'g_pre_ffn': 1.0 + nrm(21, (DEPTH, D_MODEL), 0.02),
        'g_post_ffn': 1.0 + nrm(22, (DEPTH, D_MODEL), 0.02),
        'g_mem': 1.0 + nrm(23, (DEPTH, D_MODEL), 0.02),
        'w_mem_kv': nrm(24, (DEPTH, D_MODEL, 2 * MEM_WIDTH), D_MODEL ** -0.5),
        'w_out': nrm(25, (DEPTH, SELF_WIDTH + MEM_WIDTH, D_MODEL), (SELF_WIDTH + MEM_WIDTH) ** -0.5),
        'w_gate_up': nrm(26, (DEPTH, D_MODEL, 2 * D_FF), D_MODEL ** -0.5),
        'w_down': nrm(27, (DEPTH, D_FF, D_MODEL), D_FF ** -0.5),
    }


def reference(x_prompt, x_sample, cache_fox_k, cache_fox_v, cache_fox_logf, cache_diff_k, cache_diff_v,
              cache_mem_k, cache_mem_v, page_table, mem_prompt, w_in_fox, b_f_fox, w_in_diff,
              lam_q1, lam_k1, lam_q2, lam_k2, g_subln, g_pre_mix, g_post_mix, g_pre_ffn, g_post_ffn,
              g_mem, w_mem_kv, w_out, w_gate_up, w_down):
    t_p, t_s = x_prompt.shape[1], x_sample.shape[1]
    past_len = page_table.shape[1] * PAGE_SIZE
    pos_p = jnp.arange(t_p)
    pos_s = past_len + jnp.arange(t_s)
    h_p, h_s = x_prompt, x_sample
    fk_p, fv_p, fl_p, fk_s, fv_s, fl_s = [], [], [], [], [], []
    dk_p, dv_p, dk_s, dv_s = [], [], [], []
    mk_list, mv_list = [], []

    for i in range(DEPTH):
        j = i // 2
        xn_p = rms_norm(h_p, g_pre_mix[i])
        xn_s = rms_norm(h_s, g_pre_mix[i])
        mk_p, mv_p = memory_kv(mem_prompt, g_mem[i], w_mem_kv[i])
        mk_list.append(mk_p)
        mv_list.append(mv_p)
        if i % 2 == 0:
            q, k, v, lf, qm_p = fox_project(xn_p, w_in_fox[j], b_f_fox[j])
            o_p = fox_prompt(q, k, v, lf)
            fk_p.append(k); fv_p.append(v); fl_p.append(lf)
            q, k, v, lf, qm_s = fox_project(xn_s, w_in_fox[j], b_f_fox[j])
            o_s = fox_sample(q, k, v, lf,
                             gather_pages(cache_fox_k, j, page_table),
                             gather_pages(cache_fox_v, j, page_table),
                             gather_pages(cache_fox_logf, j, page_table))
            fk_s.append(k); fv_s.append(v); fl_s.append(lf)
        else:
            lam_init = 0.8 - 0.6 * math.exp(-0.3 * i)
            lam = diff_lambda(lam_q1[j], lam_k1[j], lam_q2[j], lam_k2[j], lam_init)
            q, k, v, qm_p = diff_project(xn_p, w_in_diff[j], pos_p)
            o_p = diff_head_norm(diff_prompt(q, k, v, lam), g_subln[j], lam_init)
            dk_p.append(k); dv_p.append(v)
            q, k, v, qm_s = diff_project(xn_s, w_in_diff[j], pos_s)
            o_s = diff_sample(q, k, v,
                              gather_pages(cache_diff_k, j, page_table),
                              gather_pages(cache_diff_v, j, page_table), lam)
            o_s = diff_head_norm(o_s, g_subln[j], lam_init)
            dk_s.append(k); dv_s.append(v)
        h_p = finish_layer(h_p, o_p, qm_p, mk_p, mv_p, g_post_mix[i], w_out[i],
                           g_pre_ffn[i], g_post_ffn[i], w_gate_up[i], w_down[i])
        h_s = finish_layer(h_s, o_s, qm_s, cache_mem_k[i], cache_mem_v[i], g_post_mix[i], w_out[i],
                           g_pre_ffn[i], g_post_ffn[i], w_gate_up[i], w_down[i])

    fox_k_prompt, fox_v_prompt, fox_logf_prompt = jnp.stack(fk_p), jnp.stack(fv_p), jnp.stack(fl_p)
    fox_k_sample, fox_v_sample, fox_logf_sample = jnp.stack(fk_s), jnp.stack(fv_s), jnp.stack(fl_s)
    diff_k_prompt, diff_v_prompt = jnp.stack(dk_p), jnp.stack(dv_p)
    diff_k_sample, diff_v_sample = jnp.stack(dk_s), jnp.stack(dv_s)
    mem_k_prompt, mem_v_prompt = jnp.stack(mk_list), jnp.stack(mv_list)
    return (h_p, h_s, fox_k_prompt, fox_v_prompt, fox_logf_prompt, fox_k_sample, fox_v_sample,
            fox_logf_sample, diff_k_prompt, diff_v_prompt, diff_k_sample, diff_v_sample,
            mem_k_prompt, mem_v_prompt)
```

```python
import functools
import math

import jax
import jax.numpy as jnp
from jax import lax
from jax.experimental import pallas as pl
from jax.experimental.pallas import tpu as pltpu

F32 = jnp.float32
BF16 = jnp.bfloat16

D_MODEL = 1024
HEAD_DIM = 64
SELF_WIDTH = 768
N_HEADS_FOX = 12
N_HEADS_DIFF = 6
N_MEM = 256
MEM_WIDTH = 256
D_FF = 2816
PAGE_SIZE = 128
ROPE_THETA = 10000.0
NORM_EPS = 1e-6
NEG_INF = -1e30
QK_SCALE = HEAD_DIM ** -0.5

LANES = 128
HEAD_ROWS = 16
VMEM_LIMIT = 56 * 1024 * 1024

PROMPT_TILE = 512
LF_GROUP = 8
PAGES_PER_STEP = 8
FF_CHUNK = 1408


def _cparams(n_axes):
    return pltpu.CompilerParams(
        dimension_semantics=("arbitrary",) * n_axes, vmem_limit_bytes=VMEM_LIMIT)


def _rms(x, g):
    ms = jnp.mean(x * x, axis=-1, keepdims=True)
    return x * lax.rsqrt(ms + NORM_EPS) * g


def _split3(x):
    hi = x.astype(BF16)
    r = x - hi.astype(F32)
    mid = r.astype(BF16)
    lo = (r - mid.astype(F32)).astype(BF16)
    return hi, mid, lo


def _dot(a, b):
    return jnp.dot(a, b, preferred_element_type=F32)


def _dot_nt(a, b):
    return lax.dot_general(a, b, (((1,), (1,)), ((), ())), preferred_element_type=F32)


def _dot_exact_right(x, m01):
    hi, mid, lo = _split3(x)
    return _dot(hi, m01) + _dot(mid, m01) + _dot(lo, m01)


def _log_sigmoid(x):
    return jnp.minimum(x, 0.0) - jnp.log1p(jnp.exp(-jnp.abs(x)))


def _first_half_mask():
    return lax.broadcasted_iota(jnp.int32, (1, LANES), 1) < HEAD_DIM


def _upper_tri(n):
    row = lax.broadcasted_iota(jnp.int32, (n, n), 0)
    col = lax.broadcasted_iota(jnp.int32, (n, n), 1)
    return jnp.where(row <= col, 1.0, 0.0).astype(BF16)


def _head_selector(width):
    r = lax.broadcasted_iota(jnp.int32, (HEAD_ROWS, width), 0)
    f = lax.broadcasted_iota(jnp.int32, (HEAD_ROWS, width), 1)
    return (f // HEAD_DIM) == r


def _proj_fox_prompt_kernel(x_ref, g_ref, wq_ref, wkt_ref, wvt_ref, wm_ref, wgt_ref, bf_ref,
                            q_ref, kt_ref, vt_ref, ktb_ref, vtb_ref, qm_ref, lft_ref, negc_ref,
                            carry_ref, *, tm):
    xn = _rms(x_ref[0], g_ref[...]).astype(BF16)
    q_ref[0] = (_dot(xn, wq_ref[...]) * QK_SCALE).astype(BF16)
    qm_ref[0] = (_dot(xn, wm_ref[...]) * QK_SCALE).astype(BF16)
    for w_ref, out_ref, outb_ref in ((wkt_ref, kt_ref, ktb_ref), (wvt_ref, vt_ref, vtb_ref)):
        t = _dot_nt(w_ref[...], xn)
        out_ref[0] = t
        for p in range(SELF_WIDTH // LANES):
            outb_ref[0, p, 0] = t[p * LANES:(p + 1) * LANES, :].astype(BF16)

    gate = _dot_nt(wgt_ref[...], xn) + bf_ref[...]
    head = lax.broadcasted_iota(jnp.int32, (HEAD_ROWS, 1), 0)
    lf = jnp.where(head < N_HEADS_FOX, _log_sigmoid(gate), 0.0)
    lft_ref[0] = lf

    @pl.when(pl.program_id(1) == 0)
    def _():
        carry_ref[...] = jnp.zeros_like(carry_ref)

    cum = _dot_exact_right(lf, _upper_tri(tm)) + carry_ref[:, 0:1]
    carry_ref[...] = jnp.broadcast_to(cum[:, tm - 1:tm], carry_ref.shape)
    negc_ref[0, 0] = -cum


def _proj_fox_prompt(x, g, wq, wkt, wvt, wm, wgt, bf_col, tm):
    b, s, _ = x.shape
    n_pairs = SELF_WIDTH // LANES
    row = lambda i, j: (i, j, 0)
    col = lambda i, j: (i, 0, j)
    const = lambda i, j: (0, 0)
    blocked = lambda i, j: (i, 0, j, 0, 0)
    out_shape = (
        jax.ShapeDtypeStruct((b, s, SELF_WIDTH), BF16),
        jax.ShapeDtypeStruct((b, SELF_WIDTH, s), F32),
        jax.ShapeDtypeStruct((b, SELF_WIDTH, s), F32),
        jax.ShapeDtypeStruct((b, n_pairs, s // tm, LANES, tm), BF16),
        jax.ShapeDtypeStruct((b, n_pairs, s // tm, LANES, tm), BF16),
        jax.ShapeDtypeStruct((b, s, MEM_WIDTH), BF16),
        jax.ShapeDtypeStruct((b, HEAD_ROWS, s), F32),
        jax.ShapeDtypeStruct((b, s // tm, HEAD_ROWS, tm), F32),
    )
    out_specs = (
        pl.BlockSpec((1, tm, SELF_WIDTH), row),
        pl.BlockSpec((1, SELF_WIDTH, tm), col),
        pl.BlockSpec((1, SELF_WIDTH, tm), col),
        pl.BlockSpec((1, n_pairs, 1, LANES, tm), blocked),
        pl.BlockSpec((1, n_pairs, 1, LANES, tm), blocked),
        pl.BlockSpec((1, tm, MEM_WIDTH), row),
        pl.BlockSpec((1, HEAD_ROWS, tm), col),
        pl.BlockSpec((1, 1, HEAD_ROWS, tm), lambda i, j: (i, j, 0, 0)),
    )
    return pl.pallas_call(
        functools.partial(_proj_fox_prompt_kernel, tm=tm),
        grid=(b, s // tm),
        in_specs=[
            pl.BlockSpec((1, tm, D_MODEL), row),
            pl.BlockSpec((1, D_MODEL), const),
            pl.BlockSpec(wq.shape, const),
            pl.BlockSpec(wkt.shape, const),
            pl.BlockSpec(wvt.shape, const),
            pl.BlockSpec(wm.shape, const),
            pl.BlockSpec(wgt.shape, const),
            pl.BlockSpec(bf_col.shape, const),
        ],
        out_specs=out_specs,
        out_shape=out_shape,
        scratch_shapes=[pltpu.VMEM((HEAD_ROWS, LANES), F32)],
        compiler_params=_cparams(2),
        name="proj_fox_prompt",
    )(x, g, wq, wkt, wvt, wm, wgt, bf_col)


def _rope_chunk(x, cos, sin_signed, first_quarter):
    partner = jnp.where(first_quarter, pltpu.roll(x, 96, 1), pltpu.roll(x, 32, 1))
    return x * cos + partner * sin_signed


def _first_quarter_mask():
    lane = lax.broadcasted_iota(jnp.int32, (1, LANES), 1)
    return (lane % HEAD_DIM) < (HEAD_DIM // 2)


def _proj_diff_prompt_kernel(x_ref, g_ref, w_ref, cos_ref, sin_ref, q_ref, k_ref, v_ref, kb_ref,
                             vb_ref, qm_ref):
    w = SELF_WIDTH
    xn = _rms(x_ref[0], g_ref[...]).astype(BF16)
    cos = cos_ref[...]
    sin_signed = sin_ref[...]
    first_quarter = _first_quarter_mask()
    for h in range(N_HEADS_DIFF):
        sl = slice(h * LANES, (h + 1) * LANES)
        qc = _rope_chunk(_dot(xn, w_ref[:, sl]), cos, sin_signed, first_quarter)
        q_ref[0, :, sl] = (qc * QK_SCALE).astype(BF16)
        kc = _rope_chunk(_dot(xn, w_ref[:, w + h * LANES:w + (h + 1) * LANES]), cos, sin_signed,
                         first_quarter)
        k_ref[0, h] = kc
        kb_ref[0, h] = kc.astype(BF16)
        vc = _dot(xn, w_ref[:, 2 * w + h * LANES:2 * w + (h + 1) * LANES])
        v_ref[0, h] = vc
        vb_ref[0, h] = vc.astype(BF16)
    qm_ref[0] = (_dot(xn, w_ref[:, 3 * w:]) * QK_SCALE).astype(BF16)


def _proj_diff_prompt(x, g, w_all, cos_tab, sin_tab, tm):
    b, s, _ = x.shape
    row = lambda i, j: (i, j, 0)
    const = lambda i, j: (0, 0)
    tab = lambda i, j: (j, 0)
    heads = lambda i, j: (i, 0, j, 0)
    kv_f32 = jax.ShapeDtypeStruct((b, N_HEADS_DIFF, s, LANES), F32)
    kv_bf = jax.ShapeDtypeStruct((b, N_HEADS_DIFF, s, LANES), BF16)
    kv_spec = pl.BlockSpec((1, N_HEADS_DIFF, tm, LANES), heads)
    return pl.pallas_call(
        _proj_diff_prompt_kernel,
        grid=(b, s // tm),
        in_specs=[
            pl.BlockSpec((1, tm, D_MODEL), row),
            pl.BlockSpec((1, D_MODEL), const),
            pl.BlockSpec(w_all.shape, const),
            pl.BlockSpec((tm, LANES), tab),
            pl.BlockSpec((tm, LANES), tab),
        ],
        out_specs=(pl.BlockSpec((1, tm, SELF_WIDTH), row), kv_spec, kv_spec, kv_spec, kv_spec,
                   pl.BlockSpec((1, tm, MEM_WIDTH), row)),
        out_shape=(jax.ShapeDtypeStruct((b, s, SELF_WIDTH), BF16), kv_f32, kv_f32, kv_bf, kv_bf,
                   jax.ShapeDtypeStruct((b, s, MEM_WIDTH), BF16)),
        compiler_params=_cparams(2),
        name="proj_diff_prompt",
    )(x, g, w_all, cos_tab, sin_tab)


def _proj_sample_kernel(*refs, fox):
    if fox:
        x_ref, g_ref, w_ref, wgt_ref, bf_ref, q_ref, k_ref, v_ref, qm_ref, lft_ref = refs
    else:
        x_ref, g_ref, w_ref, cos_ref, sin_ref, q_ref, k_ref, v_ref, qm_ref = refs
    w = SELF_WIDTH
    xn = _rms(x_ref[...], g_ref[...]).astype(BF16)
    q = _dot(xn, w_ref[:, 0:w])
    k = _dot(xn, w_ref[:, w:2 * w])
    if not fox:
        first_quarter = _first_quarter_mask()
        rope = lambda t: jnp.concatenate(
            [_rope_chunk(t[:, c * LANES:(c + 1) * LANES], cos_ref[...], sin_ref[...], first_quarter)
             for c in range(w // LANES)], axis=1)
        q, k = rope(q), rope(k)
    q_ref[...] = q * QK_SCALE
    k_ref[...] = k
    v_ref[...] = _dot(xn, w_ref[:, 2 * w:3 * w])
    qm_ref[...] = _dot(xn, w_ref[:, 3 * w:3 * w + MEM_WIDTH]) * QK_SCALE
    if fox:
        gate = _dot_nt(wgt_ref[...], xn) + bf_ref[...]
        head = lax.broadcasted_iota(jnp.int32, (HEAD_ROWS, 1), 0)
        lft_ref[...] = jnp.where(head < N_HEADS_FOX, _log_sigmoid(gate), 0.0)


def _proj_sample(x, g, w_all, extras, fox):
    n = x.shape[0]
    out_shape = [jax.ShapeDtypeStruct((n, SELF_WIDTH), F32)] * 3
    out_shape.append(jax.ShapeDtypeStruct((n, MEM_WIDTH), F32))
    if fox:
        out_shape.append(jax.ShapeDtypeStruct((HEAD_ROWS, n), F32))
    return pl.pallas_call(
        functools.partial(_proj_sample_kernel, fox=fox),
        out_shape=tuple(out_shape),
        compiler_params=pltpu.CompilerParams(vmem_limit_bytes=VMEM_LIMIT),
        name="proj_fox_sample" if fox else "proj_diff_sample",
    )(x, g, w_all, *extras)


def _mem_kv_kernel(mem_ref, g_ref, wt_ref, k_ref, v_ref, kb_ref, vb_ref):
    xn = _rms(mem_ref[0], g_ref[0]).astype(BF16)
    kv = _dot_nt(wt_ref[0], xn)
    k = kv[0:MEM_WIDTH, :]
    v = kv[MEM_WIDTH:, :]
    k_ref[0, 0] = k
    v_ref[0, 0] = v
    kb_ref[0, 0] = k.astype(BF16)
    vb_ref[0, 0] = v.astype(BF16)


def _mem_kv(mem, g_mem, w_kv_t):
    depth = g_mem.shape[0]
    b = mem.shape[0]
    out_block = pl.BlockSpec((1, 1, MEM_WIDTH, N_MEM), lambda i, j: (i, j, 0, 0))
    f32_out = jax.ShapeDtypeStruct((depth, b, MEM_WIDTH, N_MEM), F32)
    bf_out = jax.ShapeDtypeStruct((depth, b, MEM_WIDTH, N_MEM), BF16)
    return pl.pallas_call(
        _mem_kv_kernel,
        grid=(depth, b),
        in_specs=[
            pl.BlockSpec((1, N_MEM, D_MODEL), lambda i, j: (j, 0, 0)),
            pl.BlockSpec((1, 1, D_MODEL), lambda i, j: (i, 0, 0)),
            pl.BlockSpec((1, 2 * MEM_WIDTH, D_MODEL), lambda i, j: (i, 0, 0)),
        ],
        out_specs=(out_block, out_block, out_block, out_block),
        out_shape=(f32_out, f32_out, bf_out, bf_out),
        compiler_params=_cparams(2),
        name="mem_kv",
    )(mem, g_mem.reshape(depth, 1, D_MODEL), w_kv_t)


def _diff_lambda(lam_ref, lam_init):
    t = lam_ref[...]
    a = jnp.sum(t[0:1] * t[1:2], axis=-1, keepdims=True)
    b = jnp.sum(t[2:3] * t[3:4], axis=-1, keepdims=True)
    return jnp.exp(a) - jnp.exp(b) + lam_init


def _head_norm(o, g, lam_init):
    ms = jnp.mean(o * o, axis=-1, keepdims=True)
    return o * lax.rsqrt(ms + NORM_EPS) * g * (1.0 - lam_init)


def _attn_prompt_kernel(*refs, fox, tile, lam_init):
    if fox:
        q_ref, kt_ref, vt_ref, negc_ref, o_ref = refs
    else:
        q_ref, k_ref, v_ref, lam_ref, g_ref, o_ref = refs
    pair = pl.program_id(1)
    qi = pl.program_id(2)
    first = _first_half_mask()
    q = q_ref[0]
    zero = jnp.zeros_like(q)
    q_halves = (jnp.where(first, q, zero), jnp.where(first, zero, q))

    def step(j, carry, masked):
        ks = pl.multiple_of(j * tile, tile)
        if fox:
            kt = kt_ref[0, 0, j]
            vt = vt_ref[0, 0, j]
        else:
            k = k_ref[0, 0, pl.ds(ks, tile), :]
            v = v_ref[0, 0, pl.ds(ks, tile), :]
        out = []
        for h in range(2):
            m, l, acc = carry[h]
            if fox:
                s = _dot(q_halves[h], kt) + negc_ref[0, j, pl.ds(2 * pair + h, 1), :]
            else:
                s = _dot_nt(q_halves[h], k)
            if masked:
                row = lax.broadcasted_iota(jnp.int32, (tile, tile), 0)
                col = lax.broadcasted_iota(jnp.int32, (tile, tile), 1)
                s = jnp.where(row >= col, s, NEG_INF)
            m_new = jnp.maximum(m, jnp.max(s, axis=-1, keepdims=True))
            alpha = jnp.exp(m - m_new)
            p = jnp.exp(s - m_new)
            l = alpha * l + jnp.sum(p, axis=-1, keepdims=True)
            pv = _dot_nt(p.astype(BF16), vt) if fox else _dot(p.astype(BF16), v)
            out.append((m_new, l, alpha * acc + pv))
        return tuple(out)

    init = tuple((jnp.full((tile, 1), NEG_INF, F32), jnp.zeros((tile, 1), F32),
                  jnp.zeros((tile, LANES), F32)) for _ in range(2))
    carry = lax.fori_loop(0, qi, lambda j, c: step(j, c, False), init)
    (_, l0, acc0), (_, l1, acc1) = step(qi, carry, True)
    if fox:
        o = jnp.where(first, acc0 / l0, acc1 / l1)
    else:
        lam = _diff_lambda(lam_ref, lam_init)
        o = _head_norm(acc0 / l0 - lam * (acc1 / l1), g_ref[...], lam_init)
    o_ref[0] = o.astype(BF16)


def _attn_prompt(q, k, v, extras, fox, tile, lam_init=0.0):
    b, s, w = q.shape
    n_blocks = w // LANES
    qo_spec = pl.BlockSpec((1, tile, LANES), lambda i, p, j: (i, j, p))
    if fox:
        kv_spec = pl.BlockSpec((1, 1, s // tile, LANES, tile), lambda i, p, j: (i, p, 0, 0, 0))
        extra_specs = [pl.BlockSpec((1, s // tile, HEAD_ROWS, tile),
                                    lambda i, p, j: (i, 0, 0, 0))]
    else:
        kv_spec = pl.BlockSpec((1, 1, s, LANES), lambda i, p, j: (i, p, 0, 0))
        extra_specs = [pl.BlockSpec((8, LANES), lambda i, p, j: (0, 0)),
                       pl.BlockSpec((1, LANES), lambda i, p, j: (0, 0))]
    return pl.pallas_call(
        functools.partial(_attn_prompt_kernel, fox=fox, tile=tile, lam_init=lam_init),
        grid=(b, n_blocks, s // tile),
        in_specs=[qo_spec, kv_spec, kv_spec] + extra_specs,
        out_specs=qo_spec,
        out_shape=jax.ShapeDtypeStruct((b, s, w), BF16),
        compiler_params=_cparams(3),
        name="attn_prompt_fox" if fox else "attn_prompt_diff",
    )(q, k, v, *extras)


def _mem_attend_block(qm, mkt, mvt, first):
    zero = jnp.zeros_like(qm)
    outs = []
    for qh in (jnp.where(first, qm, zero), jnp.where(first, zero, qm)):
        s = _dot(qh, mkt)
        p = jnp.exp(s - jnp.max(s, axis=-1, keepdims=True))
        l = jnp.sum(p, axis=-1, keepdims=True)
        outs.append(_dot_nt(p.astype(BF16), mvt) / l)
    return jnp.where(first, outs[0], outs[1])


def _mix_kernel(*refs, with_mem):
    if with_mem:
        o_ref, qm_ref, mk_ref, mv_ref, h_ref, w_ref, g_ref, out_ref = refs
        first = _first_half_mask()
        om = []
        for p in range(MEM_WIDTH // LANES):
            sl = slice(p * LANES, (p + 1) * LANES)
            om.append(_mem_attend_block(qm_ref[0, :, sl], mk_ref[0, 0, sl, :], mv_ref[0, 0, sl, :],
                                        first).astype(BF16))
        o_self = o_ref[0]
    else:
        o_ref, om_ref, h_ref, w_ref, g_ref, out_ref = refs
        o_self = o_ref[0].astype(BF16)
        om = [om_ref[0, :, p * LANES:(p + 1) * LANES].astype(BF16)
              for p in range(MEM_WIDTH // LANES)]
    y = _dot(o_self, w_ref[0:SELF_WIDTH, :])
    for p, om_p in enumerate(om):
        y = y + _dot(om_p, w_ref[SELF_WIDTH + p * LANES:SELF_WIDTH + (p + 1) * LANES, :])
    out_ref[0] = h_ref[0] + _rms(y, g_ref[...])


def _mix(o_self, mem_inputs, h, w_out, g_post, tm, with_mem, layer=0):
    b, s, _ = h.shape
    row = lambda i, j: (i, j, 0)
    const = lambda i, j: (0, 0)
    if with_mem:
        mem_spec = pl.BlockSpec((1, 1, MEM_WIDTH, N_MEM), lambda i, j: (layer, i, 0, 0))
        mem_specs = [pl.BlockSpec((1, tm, MEM_WIDTH), row), mem_spec, mem_spec]
    else:
        mem_specs = [pl.BlockSpec((1, tm, MEM_WIDTH), row)]
    return pl.pallas_call(
        functools.partial(_mix_kernel, with_mem=with_mem),
        grid=(b, s // tm),
        in_specs=[pl.BlockSpec((1, tm, SELF_WIDTH), row)] + mem_specs + [
            pl.BlockSpec((1, tm, D_MODEL), row),
            pl.BlockSpec((D_MODEL, D_MODEL), const),
            pl.BlockSpec((1, D_MODEL), const),
        ],
        out_specs=pl.BlockSpec((1, tm, D_MODEL), row),
        out_shape=jax.ShapeDtypeStruct((b, s, D_MODEL), F32),
        compiler_params=_cparams(2),
        name="mix_prompt" if with_mem else "mix_sample",
    )(o_self, *mem_inputs, h, w_out, g_post)


def _ffn_kernel(h_ref, gpre_ref, gpost_ref, wgu_ref, wd_ref, out_ref):
    h = h_ref[0]
    xn = _rms(h, gpre_ref[...]).astype(BF16)
    f = jnp.zeros(h.shape, F32)
    for c in range(D_FF // FF_CHUNK):
        lo, hi = c * FF_CHUNK, (c + 1) * FF_CHUNK
        gate = _dot(xn, wgu_ref[:, lo:hi])
        up = _dot(xn, wgu_ref[:, D_FF + lo:D_FF + hi])
        a = (gate * jax.nn.sigmoid(gate) * up).astype(BF16)
        f = f + _dot(a, wd_ref[lo:hi, :])
    out_ref[0] = h + _rms(f, gpost_ref[...])


def _ffn(h, g_pre, g_post, w_gu, w_down, tm):
    b, s, _ = h.shape
    row = lambda i, j: (i, j, 0)
    const = lambda i, j: (0, 0)
    return pl.pallas_call(
        _ffn_kernel,
        grid=(b, s // tm),
        in_specs=[
            pl.BlockSpec((1, tm, D_MODEL), row),
            pl.BlockSpec((1, D_MODEL), const),
            pl.BlockSpec((1, D_MODEL), const),
            pl.BlockSpec((D_MODEL, 2 * D_FF), const, pipeline_mode=pl.Buffered(1)),
            pl.BlockSpec((D_FF, D_MODEL), const, pipeline_mode=pl.Buffered(1)),
        ],
        out_specs=pl.BlockSpec((1, tm, D_MODEL), row),
        out_shape=jax.ShapeDtypeStruct((b, s, D_MODEL), F32),
        compiler_params=_cparams(2),
        name="ffn",
    )(h, g_pre, g_post, w_gu, w_down)


def _decode_kernel(*refs, fox, n_chunks, lam_init):
    pages = PAGES_PER_STEP
    it = iter(refs)
    pt_ref = next(it)
    q_ref, kcur_ref, vcur_ref = next(it), next(it), next(it)
    lfcur_ref = next(it) if fox else None
    qm_ref, mk_ref, mv_ref = next(it), next(it), next(it)
    if not fox:
        lam_ref, g_ref = next(it), next(it)
    k_pages = [next(it) for _ in range(pages)]
    v_pages = [next(it) for _ in range(pages)]
    lf_pages = [next(it) for _ in range(pages)] if fox else None
    o_ref, om_ref = next(it), next(it)
    m_scr, l_scr, acc_scr = next(it), next(it), next(it)
    if fox:
        carry_scr, lf_scr = next(it), next(it)

    c = pl.program_id(1)

    @pl.when(c == 0)
    def _():
        m_scr[...] = jnp.full(m_scr.shape, NEG_INF, F32)
        l_scr[...] = jnp.zeros(l_scr.shape, F32)
        acc_scr[...] = jnp.zeros(acc_scr.shape, F32)
        if fox:
            carry_scr[...] = jnp.zeros(carry_scr.shape, F32)
            lf_scr[...] = jnp.zeros(lf_scr.shape, F32)

    sel = _head_selector(SELF_WIDTH)
    q_rows = jnp.where(sel, jnp.broadcast_to(q_ref[0], (HEAD_ROWS, SELF_WIDTH)), 0.0)
    q_blk = q_rows.astype(BF16)
    if fox:
        tri = _upper_tri(PAGE_SIZE)
        carry = carry_scr[:, 0:1]
    scores = []
    for i in range(pages):
        if fox:
            s = _dot(q_blk, k_pages[i][0].astype(BF16))
            lf_row = pt_ref[pl.program_id(0), c * pages + i] % LF_GROUP
            for h in range(N_HEADS_FOX):
                lf_scr[h:h + 1, :] = lf_pages[i][h, pl.ds(lf_row, 1), :]
            cum = _dot_exact_right(lf_scr[...], tri) + carry
            carry = cum[:, PAGE_SIZE - 1:PAGE_SIZE]
            s = s - cum
        else:
            s = _dot_nt(q_blk[:, 0:LANES], k_pages[i][0, 0].astype(BF16))
            for h in range(1, N_HEADS_DIFF):
                s = s + _dot_nt(q_blk[:, h * LANES:(h + 1) * LANES], k_pages[i][0, h].astype(BF16))
        scores.append(s)
    if fox:
        carry_scr[...] = jnp.broadcast_to(carry, carry_scr.shape)
    s_all = jnp.concatenate(scores, axis=1)
    m_old = m_scr[:, 0:1]
    m_new = jnp.maximum(m_old, jnp.max(s_all, axis=1, keepdims=True))
    alpha = jnp.exp(m_old - m_new)
    p_all = jnp.exp(s_all - m_new)
    l_new = alpha * l_scr[:, 0:1] + jnp.sum(p_all, axis=1, keepdims=True)
    acc = alpha * acc_scr[...]
    for i in range(pages):
        pb = p_all[:, i * PAGE_SIZE:(i + 1) * PAGE_SIZE].astype(BF16)
        if fox:
            acc = acc + _dot_nt(pb, v_pages[i][0].astype(BF16))
        else:
            acc = acc + jnp.concatenate(
                [_dot(pb, v_pages[i][0, h].astype(BF16)) for h in range(N_HEADS_DIFF)], axis=1)
    m_scr[...] = jnp.broadcast_to(m_new, m_scr.shape)
    l_scr[...] = jnp.broadcast_to(l_new, l_scr.shape)
    acc_scr[...] = acc

    @pl.when(c == n_chunks - 1)
    def _():
        s_cur = jnp.sum(q_rows * kcur_ref[0], axis=1, keepdims=True)
        if fox:
            s_cur = s_cur - (carry + lfcur_ref[0])
        m_fin = jnp.maximum(m_new, s_cur)
        a_fin = jnp.exp(m_new - m_fin)
        p_cur = jnp.exp(s_cur - m_fin)
        l_fin = a_fin * l_new + p_cur
        o_rows = (a_fin * acc + p_cur * vcur_ref[0]) / l_fin
        if fox:
            o_ref[0] = jnp.sum(jnp.where(sel, o_rows, 0.0), axis=0, keepdims=True)
        else:
            lam = _diff_lambda(lam_ref, lam_init)
            r = lax.broadcasted_iota(jnp.int32, (HEAD_ROWS, SELF_WIDTH), 0)
            f = lax.broadcasted_iota(jnp.int32, (HEAD_ROWS, SELF_WIDTH), 1)
            head2 = 2 * (f // (2 * HEAD_DIM))
            coef = jnp.where(r == head2, 1.0, 0.0) - lam * jnp.where(r == head2 + 1, 1.0, 0.0)
            o = jnp.sum(coef * o_rows, axis=0, keepdims=True)
            for h in range(N_HEADS_DIFF):
                sl = slice(h * LANES, (h + 1) * LANES)
                o_ref[0, :, sl] = _head_norm(o[:, sl], g_ref[...], lam_init)

        sel_m = _head_selector(MEM_WIDTH)
        qm_blk = jnp.where(sel_m, jnp.broadcast_to(qm_ref[0], (HEAD_ROWS, MEM_WIDTH)), 0.0)
        s_m = _dot(qm_blk.astype(BF16), mk_ref[0, 0].astype(BF16))
        p_m = jnp.exp(s_m - jnp.max(s_m, axis=1, keepdims=True))
        l_m = jnp.sum(p_m, axis=1, keepdims=True)
        o_m = _dot_nt(p_m.astype(BF16), mv_ref[0, 0].astype(BF16)) / l_m
        om_ref[0] = jnp.sum(jnp.where(sel_m, o_m, 0.0), axis=0, keepdims=True)


def _decode(page_table, q, k_cur, v_cur, lf_cur, qm, mem_k, mem_v, layer, cache_k, cache_v,
            cache_lf, consts, fox, lam_init=0.0):
    db, n_pages = page_table.shape
    pages = PAGES_PER_STEP
    n_chunks = n_pages // pages
    seq = lambda i, c, pt: (i, 0, 0)
    const2 = lambda i, c, pt: (0, 0)

    inputs = [q, k_cur, v_cur]
    in_specs = [pl.BlockSpec((1, 1, SELF_WIDTH), seq)] * 3
    if fox:
        inputs.append(lf_cur)
        in_specs.append(pl.BlockSpec((1, HEAD_ROWS, 1), seq))
    mem_spec = pl.BlockSpec((1, 1, MEM_WIDTH, N_MEM), lambda i, c, pt: (layer, i, 0, 0))
    inputs += [qm, mem_k, mem_v]
    in_specs += [pl.BlockSpec((1, 1, MEM_WIDTH), seq), mem_spec, mem_spec]
    for cst in consts:
        inputs.append(cst)
        in_specs.append(pl.BlockSpec(cst.shape, const2))
    if fox:
        kv_specs = [pl.BlockSpec((1, SELF_WIDTH, PAGE_SIZE),
                                 lambda b, c, pt, i=i: (pt[b, c * pages + i], 0, 0))
                    for i in range(pages)]
    else:
        kv_specs = [pl.BlockSpec((1, N_HEADS_DIFF, PAGE_SIZE, LANES),
                                 lambda b, c, pt, i=i: (pt[b, c * pages + i], 0, 0, 0))
                    for i in range(pages)]
    inputs += [cache_k] * pages + [cache_v] * pages
    in_specs += kv_specs * 2
    if fox:
        inputs += [cache_lf] * pages
        in_specs += [pl.BlockSpec((N_HEADS_FOX, LF_GROUP, PAGE_SIZE),
                                  lambda b, c, pt, i=i: (0, pt[b, c * pages + i] // LF_GROUP, 0))
                     for i in range(pages)]
    scratch = [pltpu.VMEM((HEAD_ROWS, LANES), F32), pltpu.VMEM((HEAD_ROWS, LANES), F32),
               pltpu.VMEM((HEAD_ROWS, SELF_WIDTH), F32)]
    if fox:
        scratch += [pltpu.VMEM((HEAD_ROWS, LANES), F32), pltpu.VMEM((HEAD_ROWS, PAGE_SIZE), F32)]
    grid_spec = pltpu.PrefetchScalarGridSpec(
        num_scalar_prefetch=1,
        grid=(db, n_chunks),
        in_specs=in_specs,
        out_specs=(pl.BlockSpec((1, 1, SELF_WIDTH), seq), pl.BlockSpec((1, 1, MEM_WIDTH), seq)),
        scratch_shapes=scratch,
    )
    return pl.pallas_call(
        functools.partial(_decode_kernel, fox=fox, n_chunks=n_chunks, lam_init=lam_init),
        grid_spec=grid_spec,
        out_shape=(jax.ShapeDtypeStruct((db, 1, SELF_WIDTH), F32),
                   jax.ShapeDtypeStruct((db, 1, MEM_WIDTH), F32)),
        compiler_params=_cparams(2),
        name="decode_fox" if fox else "decode_diff",
    )(page_table, *inputs)


def _rope_tables(pos):
    inv = ROPE_THETA ** (-jnp.arange(0, HEAD_DIM, 2, dtype=F32) / HEAD_DIM)
    ang = pos.astype(F32)[:, None] * inv[None, :]
    cos = jnp.cos(ang)
    sin = jnp.sin(ang)
    return jnp.tile(cos, (1, 4)), jnp.tile(jnp.concatenate([-sin, sin], axis=-1), (1, 2))


def kernel(x_prompt, x_sample, cache_fox_k, cache_fox_v, cache_fox_logf, cache_diff_k, cache_diff_v, cache_mem_k, cache_mem_v, page_table, mem_prompt, w_in_fox, b_f_fox, w_in_diff, lam_q1, lam_k1, lam_q2, lam_k2, g_subln, g_pre_mix, g_post_mix, g_pre_ffn, g_post_ffn, g_mem, w_mem_kv, w_out, w_gate_up, w_down):
    depth = g_pre_mix.shape[0]
    batch, seq, _ = x_prompt.shape
    dec_batch, dec_seq, _ = x_sample.shape
    assert dec_seq == 1
    n_pool = cache_fox_k.shape[1]
    past_len = page_table.shape[1] * PAGE_SIZE
    w = SELF_WIDTH
    tm = PROMPT_TILE

    h_p = x_prompt
    h_s = x_sample.reshape(dec_batch, D_MODEL)

    mem_kt, mem_vt, mem_ktb, mem_vtb = _mem_kv(
        mem_prompt, g_mem, jnp.swapaxes(w_mem_kv, 1, 2).astype(BF16))
    mem_cache_kt = jnp.transpose(cache_mem_k, (0, 1, 3, 4, 2)).reshape(
        depth, dec_batch, MEM_WIDTH, N_MEM)
    mem_cache_vt = jnp.transpose(cache_mem_v, (0, 1, 3, 4, 2)).reshape(
        depth, dec_batch, MEM_WIDTH, N_MEM)

    fox_kt = jnp.transpose(cache_fox_k, (0, 1, 3, 4, 2)).reshape(-1, w, PAGE_SIZE)
    fox_vt = jnp.transpose(cache_fox_v, (0, 1, 3, 4, 2)).reshape(-1, w, PAGE_SIZE)
    fox_lft = jnp.transpose(cache_fox_logf, (0, 3, 1, 2))
    diff_k = jnp.transpose(cache_diff_k, (0, 1, 3, 2, 4)).reshape(-1, N_HEADS_DIFF, PAGE_SIZE, LANES)
    diff_v = jnp.transpose(cache_diff_v, (0, 1, 3, 2, 4)).reshape(-1, N_HEADS_DIFF, PAGE_SIZE, LANES)

    cos_p, sin_p = _rope_tables(jnp.arange(seq))
    cos_s, sin_s = _rope_tables(past_len + jnp.zeros((dec_batch,), jnp.int32))

    outs = {name: [] for name in ("fk_p", "fv_p", "fl_p", "fk_s", "fv_s", "fl_s",
                                  "dk_p", "dv_p", "dk_s", "dv_s")}
    for i in range(depth):
        j = i // 2
        g_pre = g_pre_mix[i].reshape(1, D_MODEL)
        w_out_b = w_out[i].astype(BF16)
        w_gu_b = w_gate_up[i].astype(BF16)
        w_down_b = w_down[i].astype(BF16)
        if i % 2 == 0:
            w_in = w_in_fox[j]
            w_in_t = w_in.T
            gate_lo, gate_hi = 3 * w, 3 * w + N_HEADS_FOX
            wgt = jnp.zeros((HEAD_ROWS, D_MODEL), F32).at[:N_HEADS_FOX].set(
                w_in_t[gate_lo:gate_hi]).astype(BF16)
            bf_col = jnp.zeros((HEAD_ROWS, 1), F32).at[:N_HEADS_FOX, 0].set(b_f_fox[j])
            w_nat = jnp.concatenate([w_in[:, :gate_lo], w_in[:, gate_hi:]], axis=1).astype(BF16)
            q, kt, vt, ktb, vtb, qm, lft, negc = _proj_fox_prompt(
                h_p, g_pre, w_nat[:, :w], w_in_t[w:2 * w].astype(BF16),
                w_in_t[2 * w:3 * w].astype(BF16), w_nat[:, 3 * w:], wgt, bf_col, tm)
            o_p = _attn_prompt(q, ktb, vtb, (negc,), fox=True, tile=tm)
            to_tokens = lambda t: jnp.transpose(
                t.reshape(batch, N_HEADS_FOX, HEAD_DIM, seq), (0, 3, 1, 2))
            outs["fk_p"].append(to_tokens(kt))
            outs["fv_p"].append(to_tokens(vt))
            outs["fl_p"].append(jnp.swapaxes(lft[:, :N_HEADS_FOX], 1, 2))

            q_s, k, v, qm_s, lft_s = _proj_sample(h_s, g_pre, w_nat, (wgt, bf_col), fox=True)
            o_s, om_s = _decode(
                page_table, q_s.reshape(dec_batch, 1, w), k.reshape(dec_batch, 1, w),
                v.reshape(dec_batch, 1, w), lft_s.T.reshape(dec_batch, HEAD_ROWS, 1),
                qm_s.reshape(dec_batch, 1, MEM_WIDTH), mem_cache_kt, mem_cache_vt, i,
                fox_kt[j * n_pool:(j + 1) * n_pool], fox_vt[j * n_pool:(j + 1) * n_pool],
                fox_lft[j], (), fox=True)
            outs["fk_s"].append(k.reshape(dec_batch, 1, N_HEADS_FOX, HEAD_DIM))
            outs["fv_s"].append(v.reshape(dec_batch, 1, N_HEADS_FOX, HEAD_DIM))
            outs["fl_s"].append(lft_s[:N_HEADS_FOX].T.reshape(dec_batch, 1, N_HEADS_FOX))
        else:
            lam_init = 0.8 - 0.6 * math.exp(-0.3 * i)
            w_all = w_in_diff[j].astype(BF16)
            lam_vec = jnp.zeros((8, LANES), F32)
            lam_vec = lam_vec.at[0, :HEAD_DIM].set(lam_q1[j]).at[1, :HEAD_DIM].set(lam_k1[j])
            lam_vec = lam_vec.at[2, :HEAD_DIM].set(lam_q2[j]).at[3, :HEAD_DIM].set(lam_k2[j])
            g_sub = g_subln[j].reshape(1, LANES)
            q, k, v, kb, vb, qm = _proj_diff_prompt(h_p, g_pre, w_all, cos_p, sin_p, tm)
            o_p = _attn_prompt(q, kb, vb, (lam_vec, g_sub), fox=False, tile=tm, lam_init=lam_init)
            outs["dk_p"].append(jnp.swapaxes(k, 1, 2))
            outs["dv_p"].append(jnp.swapaxes(v, 1, 2))

            q_s, k, v, qm_s = _proj_sample(h_s, g_pre, w_all, (cos_s, sin_s), fox=False)
            o_s, om_s = _decode(
                page_table, q_s.reshape(dec_batch, 1, w), k.reshape(dec_batch, 1, w),
                v.reshape(dec_batch, 1, w), None,
                qm_s.reshape(dec_batch, 1, MEM_WIDTH), mem_cache_kt, mem_cache_vt, i,
                diff_k[j * n_pool:(j + 1) * n_pool], diff_v[j * n_pool:(j + 1) * n_pool],
                None, (lam_vec, g_sub), fox=False, lam_init=lam_init)
            outs["dk_s"].append(k.reshape(dec_batch, 1, N_HEADS_DIFF, 2 * HEAD_DIM))
            outs["dv_s"].append(v.reshape(dec_batch, 1, N_HEADS_DIFF, 2 * HEAD_DIM))

        g_post = g_post_mix[i].reshape(1, D_MODEL)
        g_pre_f = g_pre_ffn[i].reshape(1, D_MODEL)
        g_post_f = g_post_ffn[i].reshape(1, D_MODEL)
        h_p = _mix(o_p, (qm, mem_ktb, mem_vtb), h_p, w_out_b, g_post, tm, True, layer=i)
        h_p = _ffn(h_p, g_pre_f, g_post_f, w_gu_b, w_down_b, tm)
        h_s3 = _mix(o_s.reshape(1, dec_batch, w), (om_s.reshape(1, dec_batch, MEM_WIDTH),),
                    h_s.reshape(1, dec_batch, D_MODEL), w_out_b, g_post, dec_batch, False)
        h_s = _ffn(h_s3, g_pre_f, g_post_f, w_gu_b, w_down_b, dec_batch).reshape(
            dec_batch, D_MODEL)

    st = lambda name: jnp.stack(outs[name])
    mem_tokens = lambda t: jnp.transpose(
        t.reshape(depth, batch, MEM_WIDTH // HEAD_DIM, HEAD_DIM, N_MEM), (0, 1, 4, 2, 3))
    return (h_p, h_s.reshape(dec_batch, 1, D_MODEL), st("fk_p"), st("fv_p"), st("fl_p"),
            st("fk_s"), st("fv_s"), st("fl_s"), st("dk_p"), st("dv_p"), st("dk_s"), st("dv_s"),
            mem_tokens(mem_kt), mem_tokens(mem_vt))
```

```python
import functools
import math

import jax
import jax.numpy as jnp
from jax import lax
from jax.experimental import pallas as pl
from jax.experimental.pallas import tpu as pltpu

F32 = jnp.float32
BF16 = jnp.bfloat16

D_MODEL = 1024
HEAD_DIM = 64
SELF_WIDTH = 768
N_HEADS_FOX = 12
N_HEADS_DIFF = 6
N_MEM = 256
MEM_WIDTH = 256
D_FF = 2816
PAGE_SIZE = 128
ROPE_THETA = 10000.0
NORM_EPS = 1e-6
NEG_INF = -1e30
QK_SCALE = HEAD_DIM ** -0.5
LOG2_E = math.log2(math.e)

LANES = 128
HEAD_ROWS = 16
VMEM_LIMIT = 56 * 1024 * 1024

PROMPT_TILE = 512
LF_GROUP = 8
FF_CHUNK = 1408


def _cparams(n_axes):
    return pltpu.CompilerParams(
        dimension_semantics=("arbitrary",) * n_axes, vmem_limit_bytes=VMEM_LIMIT)


def _rms(x, g):
    ms = jnp.mean(x * x, axis=-1, keepdims=True)
    return x * lax.rsqrt(ms + NORM_EPS) * g


def _split3(x):
    hi = x.astype(BF16)
    r = x - hi.astype(F32)
    mid = r.astype(BF16)
    lo = (r - mid.astype(F32)).astype(BF16)
    return hi, mid, lo


def _dot(a, b):
    return jnp.dot(a, b, preferred_element_type=F32)


def _dot_nt(a, b):
    return lax.dot_general(a, b, (((1,), (1,)), ((), ())), preferred_element_type=F32)


def _dot_exact_right(x, m01):
    hi, mid, lo = _split3(x)
    return _dot(hi, m01) + _dot(mid, m01) + _dot(lo, m01)


def _log_sigmoid(x):
    return jnp.minimum(x, 0.0) - jnp.log1p(jnp.exp(-jnp.abs(x)))


def _first_half_mask():
    return lax.broadcasted_iota(jnp.int32, (1, LANES), 1) < HEAD_DIM


def _upper_tri(n):
    row = lax.broadcasted_iota(jnp.int32, (n, n), 0)
    col = lax.broadcasted_iota(jnp.int32, (n, n), 1)
    return jnp.where(row <= col, 1.0, 0.0).astype(BF16)


def _head_selector(width):
    r = lax.broadcasted_iota(jnp.int32, (HEAD_ROWS, width), 0)
    f = lax.broadcasted_iota(jnp.int32, (HEAD_ROWS, width), 1)
    return (f // HEAD_DIM) == r


def _proj_fox_prompt_kernel(x_ref, g_ref, wq_ref, wkt_ref, wvt_ref, wm_ref, wgt_ref, bf_ref,
                            q_ref, kt_ref, vt_ref, ktb_ref, vtb_ref, qm_ref, lft_ref, negc_ref,
                            carry_ref, *, tm):
    xn = _rms(x_ref[0], g_ref[...]).astype(BF16)
    q_ref[0] = (_dot(xn, wq_ref[...]) * (QK_SCALE * LOG2_E)).astype(BF16)
    qm_ref[0] = (_dot(xn, wm_ref[...]) * QK_SCALE).astype(BF16)
    for w_ref, out_ref, outb_ref in ((wkt_ref, kt_ref, ktb_ref), (wvt_ref, vt_ref, vtb_ref)):
        t = _dot_nt(w_ref[...], xn)
        out_ref[0] = t
        for p in range(SELF_WIDTH // LANES):
            outb_ref[0, p, 0] = t[p * LANES:(p + 1) * LANES, :].astype(BF16)

    gate = _dot_nt(wgt_ref[...], xn) + bf_ref[...]
    head = lax.broadcasted_iota(jnp.int32, (HEAD_ROWS, 1), 0)
    lf = jnp.where(head < N_HEADS_FOX, _log_sigmoid(gate), 0.0)
    lft_ref[0] = lf

    @pl.when(pl.program_id(1) == 0)
    def _():
        carry_ref[...] = jnp.zeros_like(carry_ref)

    cum = _dot_exact_right(lf, _upper_tri(tm)) + carry_ref[:, 0:1]
    carry_ref[...] = jnp.broadcast_to(cum[:, tm - 1:tm], carry_ref.shape)
    negc_ref[0, 0] = cum * (-LOG2_E)


def _proj_fox_prompt(x, g, wq, wkt, wvt, wm, wgt, bf_col, tm):
    b, s, _ = x.shape
    n_pairs = SELF_WIDTH // LANES
    row = lambda i, j: (i, j, 0)
    col = lambda i, j: (i, 0, j)
    const = lambda i, j: (0, 0)
    blocked = lambda i, j: (i, 0, j, 0, 0)
    out_shape = (
        jax.ShapeDtypeStruct((b, s, SELF_WIDTH), BF16),
        jax.ShapeDtypeStruct((b, SELF_WIDTH, s), F32),
        jax.ShapeDtypeStruct((b, SELF_WIDTH, s), F32),
        jax.ShapeDtypeStruct((b, n_pairs, s // tm, LANES, tm), BF16),
        jax.ShapeDtypeStruct((b, n_pairs, s // tm, LANES, tm), BF16),
        jax.ShapeDtypeStruct((b, s, MEM_WIDTH), BF16),
        jax.ShapeDtypeStruct((b, HEAD_ROWS, s), F32),
        jax.ShapeDtypeStruct((b, s // tm, HEAD_ROWS, tm), F32),
    )
    out_specs = (
        pl.BlockSpec((1, tm, SELF_WIDTH), row),
        pl.BlockSpec((1, SELF_WIDTH, tm), col),
        pl.BlockSpec((1, SELF_WIDTH, tm), col),
        pl.BlockSpec((1, n_pairs, 1, LANES, tm), blocked),
        pl.BlockSpec((1, n_pairs, 1, LANES, tm), blocked),
        pl.BlockSpec((1, tm, MEM_WIDTH), row),
        pl.BlockSpec((1, HEAD_ROWS, tm), col),
        pl.BlockSpec((1, 1, HEAD_ROWS, tm), lambda i, j: (i, j, 0, 0)),
    )
    return pl.pallas_call(
        functools.partial(_proj_fox_prompt_kernel, tm=tm),
        grid=(b, s // tm),
        in_specs=[
            pl.BlockSpec((1, tm, D_MODEL), row),
            pl.BlockSpec((1, D_MODEL), const),
            pl.BlockSpec(wq.shape, const),
            pl.BlockSpec(wkt.shape, const),
            pl.BlockSpec(wvt.shape, const),
            pl.BlockSpec(wm.shape, const),
            pl.BlockSpec(wgt.shape, const),
            pl.BlockSpec(bf_col.shape, const),
        ],
        out_specs=out_specs,
        out_shape=out_shape,
        scratch_shapes=[pltpu.VMEM((HEAD_ROWS, LANES), F32)],
        compiler_params=_cparams(2),
        name="proj_fox_prompt",
    )(x, g, wq, wkt, wvt, wm, wgt, bf_col)


def _rope_chunk(x, cos, sin_signed, first_quarter):
    partner = jnp.where(first_quarter, pltpu.roll(x, 96, 1), pltpu.roll(x, 32, 1))
    return x * cos + partner * sin_signed


def _first_quarter_mask():
    lane = lax.broadcasted_iota(jnp.int32, (1, LANES), 1)
    return (lane % HEAD_DIM) < (HEAD_DIM // 2)


def _proj_diff_prompt_kernel(x_ref, g_ref, w_ref, cos_ref, sin_ref, q_ref, k_ref, v_ref, kb_ref,
                             vb_ref, qm_ref):
    w = SELF_WIDTH
    xn = _rms(x_ref[0], g_ref[...]).astype(BF16)
    cos = cos_ref[...]
    sin_signed = sin_ref[...]
    first_quarter = _first_quarter_mask()
    for h in range(N_HEADS_DIFF):
        sl = slice(h * LANES, (h + 1) * LANES)
        qc = _rope_chunk(_dot(xn, w_ref[:, sl]), cos, sin_signed, first_quarter)
        q_ref[0, :, sl] = (qc * (QK_SCALE * LOG2_E)).astype(BF16)
        kc = _rope_chunk(_dot(xn, w_ref[:, w + h * LANES:w + (h + 1) * LANES]), cos, sin_signed,
                         first_quarter)
        k_ref[0, h] = kc
        kb_ref[0, h] = kc.astype(BF16)
        vc = _dot(xn, w_ref[:, 2 * w + h * LANES:2 * w + (h + 1) * LANES])
        v_ref[0, h] = vc
        vb_ref[0, h] = vc.astype(BF16)
    qm_ref[0] = (_dot(xn, w_ref[:, 3 * w:]) * QK_SCALE).astype(BF16)


def _proj_diff_prompt(x, g, w_all, cos_tab, sin_tab, tm):
    b, s, _ = x.shape
    row = lambda i, j: (i, j, 0)
    const = lambda i, j: (0, 0)
    tab = lambda i, j: (j, 0)
    heads = lambda i, j: (i, 0, j, 0)
    kv_f32 = jax.ShapeDtypeStruct((b, N_HEADS_DIFF, s, LANES), F32)
    kv_bf = jax.ShapeDtypeStruct((b, N_HEADS_DIFF, s, LANES), BF16)
    kv_spec = pl.BlockSpec((1, N_HEADS_DIFF, tm, LANES), heads)
    return pl.pallas_call(
        _proj_diff_prompt_kernel,
        grid=(b, s // tm),
        in_specs=[
            pl.BlockSpec((1, tm, D_MODEL), row),
            pl.BlockSpec((1, D_MODEL), const),
            pl.BlockSpec(w_all.shape, const),
            pl.BlockSpec((tm, LANES), tab),
            pl.BlockSpec((tm, LANES), tab),
        ],
        out_specs=(pl.BlockSpec((1, tm, SELF_WIDTH), row), kv_spec, kv_spec, kv_spec, kv_spec,
                   pl.BlockSpec((1, tm, MEM_WIDTH), row)),
        out_shape=(jax.ShapeDtypeStruct((b, s, SELF_WIDTH), BF16), kv_f32, kv_f32, kv_bf, kv_bf,
                   jax.ShapeDtypeStruct((b, s, MEM_WIDTH), BF16)),
        compiler_params=_cparams(2),
        name="proj_diff_prompt",
    )(x, g, w_all, cos_tab, sin_tab)


def _proj_sample_kernel(*refs, fox):
    if fox:
        x_ref, g_ref, w_ref, wgt_ref, bf_ref, q_ref, k_ref, v_ref, qm_ref, lft_ref = refs
    else:
        x_ref, g_ref, w_ref, cos_ref, sin_ref, q_ref, k_ref, v_ref, qm_ref = refs
    w = SELF_WIDTH
    xn = _rms(x_ref[...], g_ref[...]).astype(BF16)
    q = _dot(xn, w_ref[:, 0:w])
    k = _dot(xn, w_ref[:, w:2 * w])
    if not fox:
        first_quarter = _first_quarter_mask()
        rope = lambda t: jnp.concatenate(
            [_rope_chunk(t[:, c * LANES:(c + 1) * LANES], cos_ref[...], sin_ref[...], first_quarter)
             for c in range(w // LANES)], axis=1)
        q, k = rope(q), rope(k)
    q_ref[...] = q * QK_SCALE
    k_ref[...] = k
    v_ref[...] = _dot(xn, w_ref[:, 2 * w:3 * w])
    qm_ref[...] = _dot(xn, w_ref[:, 3 * w:3 * w + MEM_WIDTH]) * QK_SCALE
    if fox:
        gate = _dot_nt(wgt_ref[...], xn) + bf_ref[...]
        head = lax.broadcasted_iota(jnp.int32, (HEAD_ROWS, 1), 0)
        lft_ref[...] = jnp.where(head < N_HEADS_FOX, _log_sigmoid(gate), 0.0)


def _proj_sample(x, g, w_all, extras, fox):
    n = x.shape[0]
    out_shape = [jax.ShapeDtypeStruct((n, SELF_WIDTH), F32)] * 3
    out_shape.append(jax.ShapeDtypeStruct((n, MEM_WIDTH), F32))
    if fox:
        out_shape.append(jax.ShapeDtypeStruct((HEAD_ROWS, n), F32))
    return pl.pallas_call(
        functools.partial(_proj_sample_kernel, fox=fox),
        out_shape=tuple(out_shape),
        compiler_params=pltpu.CompilerParams(vmem_limit_bytes=VMEM_LIMIT),
        name="proj_fox_sample" if fox else "proj_diff_sample",
    )(x, g, w_all, *extras)


def _mem_kv_kernel(mem_ref, g_ref, wt_ref, k_ref, v_ref, kb_ref, vb_ref):
    xn = _rms(mem_ref[0], g_ref[0]).astype(BF16)
    kv = _dot_nt(wt_ref[0], xn)
    k = kv[0:MEM_WIDTH, :]
    v = kv[MEM_WIDTH:, :]
    k_ref[0, 0] = k
    v_ref[0, 0] = v
    kb_ref[0, 0] = k.astype(BF16)
    vb_ref[0, 0] = v.astype(BF16)


def _mem_kv(mem, g_mem, w_kv_t):
    depth = g_mem.shape[0]
    b = mem.shape[0]
    out_block = pl.BlockSpec((1, 1, MEM_WIDTH, N_MEM), lambda i, j: (i, j, 0, 0))
    f32_out = jax.ShapeDtypeStruct((depth, b, MEM_WIDTH, N_MEM), F32)
    bf_out = jax.ShapeDtypeStruct((depth, b, MEM_WIDTH, N_MEM), BF16)
    return pl.pallas_call(
        _mem_kv_kernel,
        grid=(depth, b),
        in_specs=[
            pl.BlockSpec((1, N_MEM, D_MODEL), lambda i, j: (j, 0, 0)),
            pl.BlockSpec((1, 1, D_MODEL), lambda i, j: (i, 0, 0)),
            pl.BlockSpec((1, 2 * MEM_WIDTH, D_MODEL), lambda i, j: (i, 0, 0)),
        ],
        out_specs=(out_block, out_block, out_block, out_block),
        out_shape=(f32_out, f32_out, bf_out, bf_out),
        compiler_params=_cparams(2),
        name="mem_kv",
    )(mem, g_mem.reshape(depth, 1, D_MODEL), w_kv_t)


def _diff_lambda(lam_ref, lam_init):
    t = lam_ref[...]
    a = jnp.sum(t[0:1] * t[1:2], axis=-1, keepdims=True)
    b = jnp.sum(t[2:3] * t[3:4], axis=-1, keepdims=True)
    return jnp.exp(a) - jnp.exp(b) + lam_init


def _head_norm(o, g, lam_init):
    ms = jnp.mean(o * o, axis=-1, keepdims=True)
    return o * lax.rsqrt(ms + NORM_EPS) * g * (1.0 - lam_init)


def _prompt_attention(q_ref, k_ref, v_ref, extra_refs, o_ref, *, fox, tile, lam_init, pair, qi):
    first = _first_half_mask()
    q = q_ref[0]
    zero = jnp.zeros_like(q)
    q_halves = (jnp.where(first, q, zero), jnp.where(first, zero, q))

    def step(j, carry, masked):
        if fox:
            k = k_ref[0, 0, j]
            v = v_ref[0, 0, j]
        else:
            ks = pl.multiple_of(j * tile, tile)
            k = k_ref[0, 0, pl.ds(ks, tile), :]
            v = v_ref[0, 0, pl.ds(ks, tile), :]
        out = []
        for h in range(2):
            m, l, acc = carry[h]
            if fox:
                s = _dot(q_halves[h], k) + extra_refs[0][0, j, pl.ds(2 * pair + h, 1), :]
            else:
                s = _dot_nt(q_halves[h], k)
            if masked:
                row = lax.broadcasted_iota(jnp.int32, (tile, tile), 0)
                col = lax.broadcasted_iota(jnp.int32, (tile, tile), 1)
                s = jnp.where(row >= col, s, NEG_INF)
            m_new = jnp.maximum(m, jnp.max(s, axis=-1, keepdims=True))
            alpha = jnp.exp2(m - m_new)
            p = jnp.exp2(s - m_new)
            l = alpha * l + jnp.sum(p, axis=-1, keepdims=True)
            pv = _dot_nt(p.astype(BF16), v) if fox else _dot(p.astype(BF16), v)
            out.append((m_new, l, alpha * acc + pv))
        return tuple(out)

    init = tuple((jnp.full((tile, 1), NEG_INF, F32), jnp.zeros((tile, 1), F32),
                  jnp.zeros((tile, LANES), F32)) for _ in range(2))
    carry = lax.fori_loop(0, qi, lambda j, c: step(j, c, False), init)
    (_, l0, acc0), (_, l1, acc1) = step(qi, carry, True)
    if fox:
        o = jnp.where(first, acc0 / l0, acc1 / l1)
    else:
        lam_ref, g_ref = extra_refs
        lam = _diff_lambda(lam_ref, lam_init)
        o = _head_norm(acc0 / l0 - lam * (acc1 / l1), g_ref[...], lam_init)
    o_ref[0] = o.astype(BF16)


def _mem_attend_block(qm, mkt, mvt, first):
    zero = jnp.zeros_like(qm)
    outs = []
    for qh in (jnp.where(first, qm, zero), jnp.where(first, zero, qm)):
        s = _dot(qh, mkt)
        p = jnp.exp(s - jnp.max(s, axis=-1, keepdims=True))
        l = jnp.sum(p, axis=-1, keepdims=True)
        outs.append(_dot_nt(p.astype(BF16), mvt) / l)
    return jnp.where(first, outs[0], outs[1])


def _mix_kernel(*refs, with_mem):
    if with_mem:
        o_ref, qm_ref, mk_ref, mv_ref, h_ref, w_ref, g_ref, out_ref = refs
        first = _first_half_mask()
        om = []
        for p in range(MEM_WIDTH // LANES):
            sl = slice(p * LANES, (p + 1) * LANES)
            om.append(_mem_attend_block(qm_ref[0, :, sl], mk_ref[0, 0, sl, :], mv_ref[0, 0, sl, :],
                                        first).astype(BF16))
        o_self = o_ref[0]
    else:
        o_ref, om_ref, h_ref, w_ref, g_ref, out_ref = refs
        o_self = o_ref[0].astype(BF16)
        om = [om_ref[0, :, p * LANES:(p + 1) * LANES].astype(BF16)
              for p in range(MEM_WIDTH // LANES)]
    y = _dot(o_self, w_ref[0:SELF_WIDTH, :])
    for p, om_p in enumerate(om):
        y = y + _dot(om_p, w_ref[SELF_WIDTH + p * LANES:SELF_WIDTH + (p + 1) * LANES, :])
    out_ref[0] = h_ref[0] + _rms(y, g_ref[...])


def _mix(o_self, mem_inputs, h, w_out, g_post, tm, with_mem, layer=0):
    b, s, _ = h.shape
    row = lambda i, j: (i, j, 0)
    const = lambda i, j: (0, 0)
    if with_mem:
        mem_spec = pl.BlockSpec((1, 1, MEM_WIDTH, N_MEM), lambda i, j: (layer, i, 0, 0))
        mem_specs = [pl.BlockSpec((1, tm, MEM_WIDTH), row), mem_spec, mem_spec]
    else:
        mem_specs = [pl.BlockSpec((1, tm, MEM_WIDTH), row)]
    return pl.pallas_call(
        functools.partial(_mix_kernel, with_mem=with_mem),
        grid=(b, s // tm),
        in_specs=[pl.BlockSpec((1, tm, SELF_WIDTH), row)] + mem_specs + [
            pl.BlockSpec((1, tm, D_MODEL), row),
            pl.BlockSpec((D_MODEL, D_MODEL), const),
            pl.BlockSpec((1, D_MODEL), const),
        ],
        out_specs=pl.BlockSpec((1, tm, D_MODEL), row),
        out_shape=jax.ShapeDtypeStruct((b, s, D_MODEL), F32),
        compiler_params=_cparams(2),
        name="mix_prompt" if with_mem else "mix_sample",
    )(o_self, *mem_inputs, h, w_out, g_post)


def _ffn_kernel(h_ref, gpre_ref, gpost_ref, wgu_ref, wd_ref, out_ref):
    h = h_ref[0]
    xn = _rms(h, gpre_ref[...]).astype(BF16)
    f = jnp.zeros(h.shape, F32)
    for c in range(D_FF // FF_CHUNK):
        lo, hi = c * FF_CHUNK, (c + 1) * FF_CHUNK
        gate = _dot(xn, wgu_ref[:, lo:hi])
        up = _dot(xn, wgu_ref[:, D_FF + lo:D_FF + hi])
        a = (gate * jax.nn.sigmoid(gate) * up).astype(BF16)
        f = f + _dot(a, wd_ref[lo:hi, :])
    out_ref[0] = h + _rms(f, gpost_ref[...])


def _ffn(h, g_pre, g_post, w_gu, w_down, tm):
    b, s, _ = h.shape
    row = lambda i, j: (i, j, 0)
    const = lambda i, j: (0, 0)
    return pl.pallas_call(
        _ffn_kernel,
        grid=(b, s // tm),
        in_specs=[
            pl.BlockSpec((1, tm, D_MODEL), row),
            pl.BlockSpec((1, D_MODEL), const),
            pl.BlockSpec((1, D_MODEL), const),
            pl.BlockSpec((D_MODEL, 2 * D_FF), const, pipeline_mode=pl.Buffered(1)),
            pl.BlockSpec((D_FF, D_MODEL), const, pipeline_mode=pl.Buffered(1)),
        ],
        out_specs=pl.BlockSpec((1, tm, D_MODEL), row),
        out_shape=jax.ShapeDtypeStruct((b, s, D_MODEL), F32),
        compiler_params=_cparams(2),
        name="ffn",
    )(h, g_pre, g_post, w_gu, w_down)


def _decode_step(pt_ref, seq, c, q_ref, kcur_ref, vcur_ref, lfcur_ref, qm_ref, mk_ref, mv_ref,
                 lam_ref, g_ref, k_pages, v_pages, lf_pages, o_ref, om_ref, scratch, *, fox,
                 n_chunks, n_pages, lam_init):
    pages = len(k_pages)
    if fox:
        m_scr, l_scr, acc_scr, carry_scr, lf_scr = scratch
    else:
        m_scr, l_scr, acc_scr = scratch

    @pl.when(c == 0)
    def _():
        m_scr[...] = jnp.full(m_scr.shape, NEG_INF, F32)
        l_scr[...] = jnp.zeros(l_scr.shape, F32)
        acc_scr[...] = jnp.zeros(acc_scr.shape, F32)
        if fox:
            carry_scr[...] = jnp.zeros(carry_scr.shape, F32)
            lf_scr[...] = jnp.zeros(lf_scr.shape, F32)

    sel = _head_selector(SELF_WIDTH)
    q_rows = jnp.where(sel, jnp.broadcast_to(q_ref[0], (HEAD_ROWS, SELF_WIDTH)), 0.0)
    q_blk = q_rows.astype(BF16)
    if fox:
        tri = _upper_tri(PAGE_SIZE)
        carry = carry_scr[:, 0:1]
    scores = []
    for i in range(pages):
        always_valid = (n_chunks - 1) * pages + i < n_pages
        valid = None if always_valid else (c * pages + i < n_pages)
        if fox:
            s = _dot(q_blk, k_pages[i][0].astype(BF16))
            page = pt_ref[seq, jnp.minimum(c * pages + i, n_pages - 1)]
            lf_row = page % LF_GROUP
            for h in range(N_HEADS_FOX):
                lf_scr[h:h + 1, :] = lf_pages[i][h, pl.ds(lf_row, 1), :]
            lf = lf_scr[...] if always_valid else jnp.where(valid, lf_scr[...], 0.0)
            cum = _dot_exact_right(lf, tri) + carry
            carry = cum[:, PAGE_SIZE - 1:PAGE_SIZE]
            s = s - cum
        else:
            s = _dot_nt(q_blk[:, 0:LANES], k_pages[i][0, 0].astype(BF16))
            for h in range(1, N_HEADS_DIFF):
                s = s + _dot_nt(q_blk[:, h * LANES:(h + 1) * LANES], k_pages[i][0, h].astype(BF16))
        if not always_valid:
            s = jnp.where(valid, s, NEG_INF)
        scores.append(s)
    if fox:
        carry_scr[...] = jnp.broadcast_to(carry, carry_scr.shape)
    s_all = jnp.concatenate(scores, axis=1)
    m_old = m_scr[:, 0:1]
    m_new = jnp.maximum(m_old, jnp.max(s_all, axis=1, keepdims=True))
    alpha = jnp.exp(m_old - m_new)
    p_all = jnp.exp(s_all - m_new)
    l_new = alpha * l_scr[:, 0:1] + jnp.sum(p_all, axis=1, keepdims=True)
    acc = alpha * acc_scr[...]
    for i in range(pages):
        pb = p_all[:, i * PAGE_SIZE:(i + 1) * PAGE_SIZE].astype(BF16)
        if fox:
            acc = acc + _dot_nt(pb, v_pages[i][0].astype(BF16))
        else:
            acc = acc + jnp.concatenate(
                [_dot(pb, v_pages[i][0, h].astype(BF16)) for h in range(N_HEADS_DIFF)], axis=1)
    m_scr[...] = jnp.broadcast_to(m_new, m_scr.shape)
    l_scr[...] = jnp.broadcast_to(l_new, l_scr.shape)
    acc_scr[...] = acc

    @pl.when(c == n_chunks - 1)
    def _():
        s_cur = jnp.sum(q_rows * kcur_ref[0], axis=1, keepdims=True)
        if fox:
            s_cur = s_cur - (carry + lfcur_ref[0])
        m_fin = jnp.maximum(m_new, s_cur)
        a_fin = jnp.exp(m_new - m_fin)
        p_cur = jnp.exp(s_cur - m_fin)
        l_fin = a_fin * l_new + p_cur
        o_rows = (a_fin * acc + p_cur * vcur_ref[0]) / l_fin
        if fox:
            o_ref[0] = jnp.sum(jnp.where(sel, o_rows, 0.0), axis=0, keepdims=True)
        else:
            lam = _diff_lambda(lam_ref, lam_init)
            r = lax.broadcasted_iota(jnp.int32, (HEAD_ROWS, SELF_WIDTH), 0)
            f = lax.broadcasted_iota(jnp.int32, (HEAD_ROWS, SELF_WIDTH), 1)
            head2 = 2 * (f // (2 * HEAD_DIM))
            coef = jnp.where(r == head2, 1.0, 0.0) - lam * jnp.where(r == head2 + 1, 1.0, 0.0)
            o = jnp.sum(coef * o_rows, axis=0, keepdims=True)
            for h in range(N_HEADS_DIFF):
                sl = slice(h * LANES, (h + 1) * LANES)
                o_ref[0, :, sl] = _head_norm(o[:, sl], g_ref[...], lam_init)

        sel_m = _head_selector(MEM_WIDTH)
        qm_blk = jnp.where(sel_m, jnp.broadcast_to(qm_ref[0], (HEAD_ROWS, MEM_WIDTH)), 0.0)
        s_m = _dot(qm_blk.astype(BF16), mk_ref[0, 0].astype(BF16))
        p_m = jnp.exp(s_m - jnp.max(s_m, axis=1, keepdims=True))
        l_m = jnp.sum(p_m, axis=1, keepdims=True)
        o_m = _dot_nt(p_m.astype(BF16), mv_ref[0, 0].astype(BF16)) / l_m
        om_ref[0] = jnp.sum(jnp.where(sel_m, o_m, 0.0), axis=0, keepdims=True)


def _attn_kernel(*refs, fox, tile, lam_init, pages, n_chunks, n_pages):
    it = iter(refs)
    pt_ref = next(it)
    q_ref, k_ref, v_ref = next(it), next(it), next(it)
    extra_refs = (next(it),) if fox else (next(it), next(it))
    qs_ref, kcur_ref, vcur_ref = next(it), next(it), next(it)
    lfcur_ref = next(it) if fox else None
    qm_ref, mk_ref, mv_ref = next(it), next(it), next(it)
    k_pages = [next(it) for _ in range(pages)]
    v_pages = [next(it) for _ in range(pages)]
    lf_pages = [next(it) for _ in range(pages)] if fox else None
    o_ref, os_ref, om_ref = next(it), next(it), next(it)
    scratch = list(it)

    pair, qi = pl.program_id(1), pl.program_id(2)
    step = (pl.program_id(0) * pl.num_programs(1) + pair) * pl.num_programs(2) + qi
    lam_ref, g_ref = (None, None) if fox else extra_refs
    _decode_step(pt_ref, step // n_chunks, step % n_chunks, qs_ref, kcur_ref, vcur_ref, lfcur_ref,
                 qm_ref, mk_ref, mv_ref, lam_ref, g_ref, k_pages, v_pages, lf_pages, os_ref, om_ref,
                 scratch, fox=fox, n_chunks=n_chunks, n_pages=n_pages, lam_init=lam_init)
    _prompt_attention(q_ref, k_ref, v_ref, extra_refs, o_ref, fox=fox, tile=tile,
                      lam_init=lam_init, pair=pair, qi=qi)


def _attn(q, k, v, extras, page_table, q_s, k_cur, v_cur, lf_cur, qm_s, mem_k, mem_v, layer,
          cache_k, cache_v, cache_lf, fox, tile, lam_init=0.0):
    b, s, w = q.shape
    n_blocks = w // LANES
    nq = s // tile
    db, n_pages = page_table.shape
    n_steps = b * n_blocks * nq
    assert n_steps % db == 0
    n_chunks = n_steps // db
    pages = -(-n_pages // n_chunks)

    def seq_of(i, p, j):
        return ((i * n_blocks + p) * nq + j) // n_chunks

    def page_of(i, p, j, pt, slot):
        step = (i * n_blocks + p) * nq + j
        return pt[step // n_chunks, jnp.minimum((step % n_chunks) * pages + slot, n_pages - 1)]

    qo_spec = pl.BlockSpec((1, tile, LANES), lambda i, p, j, pt: (i, j, p))
    if fox:
        kv_spec = pl.BlockSpec((1, 1, nq, LANES, tile), lambda i, p, j, pt: (i, p, 0, 0, 0))
        extra_specs = [pl.BlockSpec((1, nq, HEAD_ROWS, tile), lambda i, p, j, pt: (i, 0, 0, 0))]
    else:
        kv_spec = pl.BlockSpec((1, 1, s, LANES), lambda i, p, j, pt: (i, p, 0, 0))
        extra_specs = [pl.BlockSpec((8, LANES), lambda i, p, j, pt: (0, 0)),
                       pl.BlockSpec((1, LANES), lambda i, p, j, pt: (0, 0))]
    inputs = [q, k, v, *extras]
    in_specs = [qo_spec, kv_spec, kv_spec] + extra_specs

    seq3 = lambda i, p, j, pt: (seq_of(i, p, j), 0, 0)
    inputs += [q_s, k_cur, v_cur]
    in_specs += [pl.BlockSpec((1, 1, SELF_WIDTH), seq3)] * 3
    if fox:
        inputs.append(lf_cur)
        in_specs.append(pl.BlockSpec((1, HEAD_ROWS, 1), seq3))
    mem_spec = pl.BlockSpec((1, 1, MEM_WIDTH, N_MEM),
                            lambda i, p, j, pt: (layer, seq_of(i, p, j), 0, 0))
    inputs += [qm_s, mem_k, mem_v]
    in_specs += [pl.BlockSpec((1, 1, MEM_WIDTH), seq3), mem_spec, mem_spec]
    if fox:
        page_specs = [pl.BlockSpec((1, SELF_WIDTH, PAGE_SIZE),
                                   lambda i, p, j, pt, t=t: (page_of(i, p, j, pt, t), 0, 0))
                      for t in range(pages)]
    else:
        page_specs = [pl.BlockSpec((1, N_HEADS_DIFF, PAGE_SIZE, LANES),
                                   lambda i, p, j, pt, t=t: (page_of(i, p, j, pt, t), 0, 0, 0))
                      for t in range(pages)]
    inputs += [cache_k] * pages + [cache_v] * pages
    in_specs += page_specs * 2
    if fox:
        inputs += [cache_lf] * pages
        in_specs += [pl.BlockSpec(
            (N_HEADS_FOX, LF_GROUP, PAGE_SIZE),
            lambda i, p, j, pt, t=t: (0, page_of(i, p, j, pt, t) // LF_GROUP, 0))
            for t in range(pages)]
    scratch = [pltpu.VMEM((HEAD_ROWS, LANES), F32), pltpu.VMEM((HEAD_ROWS, LANES), F32),
               pltpu.VMEM((HEAD_ROWS, SELF_WIDTH), F32)]
    if fox:
        scratch += [pltpu.VMEM((HEAD_ROWS, LANES), F32), pltpu.VMEM((HEAD_ROWS, PAGE_SIZE), F32)]
    grid_spec = pltpu.PrefetchScalarGridSpec(
        num_scalar_prefetch=1,
        grid=(b, n_blocks, nq),
        in_specs=in_specs,
        out_specs=(qo_spec, pl.BlockSpec((1, 1, SELF_WIDTH), seq3),
                   pl.BlockSpec((1, 1, MEM_WIDTH), seq3)),
        scratch_shapes=scratch,
    )
    return pl.pallas_call(
        functools.partial(_attn_kernel, fox=fox, tile=tile, lam_init=lam_init, pages=pages,
                          n_chunks=n_chunks, n_pages=n_pages),
        grid_spec=grid_spec,
        out_shape=(jax.ShapeDtypeStruct((b, s, w), BF16),
                   jax.ShapeDtypeStruct((db, 1, SELF_WIDTH), F32),
                   jax.ShapeDtypeStruct((db, 1, MEM_WIDTH), F32)),
        compiler_params=_cparams(3),
        name="attn_fox" if fox else "attn_diff",
    )(page_table, *inputs)


def _rope_tables(pos):
    inv = ROPE_THETA ** (-jnp.arange(0, HEAD_DIM, 2, dtype=F32) / HEAD_DIM)
    ang = pos.astype(F32)[:, None] * inv[None, :]
    cos = jnp.cos(ang)
    sin = jnp.sin(ang)
    return jnp.tile(cos, (1, 4)), jnp.tile(jnp.concatenate([-sin, sin], axis=-1), (1, 2))


def kernel(x_prompt, x_sample, cache_fox_k, cache_fox_v, cache_fox_logf, cache_diff_k, cache_diff_v, cache_mem_k, cache_mem_v, page_table, mem_prompt, w_in_fox, b_f_fox, w_in_diff, lam_q1, lam_k1, lam_q2, lam_k2, g_subln, g_pre_mix, g_post_mix, g_pre_ffn, g_post_ffn, g_mem, w_mem_kv, w_out, w_gate_up, w_down):
    depth = g_pre_mix.shape[0]
    batch, seq, _ = x_prompt.shape
    dec_batch, dec_seq, _ = x_sample.shape
    assert dec_seq == 1
    n_pool = cache_fox_k.shape[1]
    past_len = page_table.shape[1] * PAGE_SIZE
    w = SELF_WIDTH
    tm = PROMPT_TILE

    h_p = x_prompt
    h_s = x_sample.reshape(dec_batch, D_MODEL)

    mem_kt, mem_vt, mem_ktb, mem_vtb = _mem_kv(
        mem_prompt, g_mem, jnp.swapaxes(w_mem_kv, 1, 2).astype(BF16))
    mem_cache_kt = jnp.transpose(cache_mem_k, (0, 1, 3, 4, 2)).reshape(
        depth, dec_batch, MEM_WIDTH, N_MEM)
    mem_cache_vt = jnp.transpose(cache_mem_v, (0, 1, 3, 4, 2)).reshape(
        depth, dec_batch, MEM_WIDTH, N_MEM)

    fox_kt = jnp.transpose(cache_fox_k, (0, 1, 3, 4, 2)).reshape(-1, w, PAGE_SIZE)
    fox_vt = jnp.transpose(cache_fox_v, (0, 1, 3, 4, 2)).reshape(-1, w, PAGE_SIZE)
    fox_lft = jnp.transpose(cache_fox_logf, (0, 3, 1, 2))
    diff_k = jnp.transpose(cache_diff_k, (0, 1, 3, 2, 4)).reshape(-1, N_HEADS_DIFF, PAGE_SIZE, LANES)
    diff_v = jnp.transpose(cache_diff_v, (0, 1, 3, 2, 4)).reshape(-1, N_HEADS_DIFF, PAGE_SIZE, LANES)

    cos_p, sin_p = _rope_tables(jnp.arange(seq))
    cos_s, sin_s = _rope_tables(past_len + jnp.zeros((dec_batch,), jnp.int32))

    outs = {name: [] for name in ("fk_p", "fv_p", "fl_p", "fk_s", "fv_s", "fl_s",
                                  "dk_p", "dv_p", "dk_s", "dv_s")}
    for i in range(depth):
        j = i // 2
        g_pre = g_pre_mix[i].reshape(1, D_MODEL)
        w_out_b = w_out[i].astype(BF16)
        w_gu_b = w_gate_up[i].astype(BF16)
        w_down_b = w_down[i].astype(BF16)
        if i % 2 == 0:
            w_in = w_in_fox[j]
            w_in_t = w_in.T
            gate_lo, gate_hi = 3 * w, 3 * w + N_HEADS_FOX
            wgt = jnp.zeros((HEAD_ROWS, D_MODEL), F32).at[:N_HEADS_FOX].set(
                w_in_t[gate_lo:gate_hi]).astype(BF16)
            bf_col = jnp.zeros((HEAD_ROWS, 1), F32).at[:N_HEADS_FOX, 0].set(b_f_fox[j])
            w_nat = jnp.concatenate([w_in[:, :gate_lo], w_in[:, gate_hi:]], axis=1).astype(BF16)
            q, kt, vt, ktb, vtb, qm, lft, negc = _proj_fox_prompt(
                h_p, g_pre, w_nat[:, :w], w_in_t[w:2 * w].astype(BF16),
                w_in_t[2 * w:3 * w].astype(BF16), w_nat[:, 3 * w:], wgt, bf_col, tm)
            to_tokens = lambda t: jnp.transpose(
                t.reshape(batch, N_HEADS_FOX, HEAD_DIM, seq), (0, 3, 1, 2))
            outs["fk_p"].append(to_tokens(kt))
            outs["fv_p"].append(to_tokens(vt))
            outs["fl_p"].append(jnp.swapaxes(lft[:, :N_HEADS_FOX], 1, 2))

            q_s, k, v, qm_s, lft_s = _proj_sample(h_s, g_pre, w_nat, (wgt, bf_col), fox=True)
            o_p, o_s, om_s = _attn(
                q, ktb, vtb, (negc,), page_table,
                q_s.reshape(dec_batch, 1, w), k.reshape(dec_batch, 1, w),
                v.reshape(dec_batch, 1, w), lft_s.T.reshape(dec_batch, HEAD_ROWS, 1),
                qm_s.reshape(dec_batch, 1, MEM_WIDTH), mem_cache_kt, mem_cache_vt, i,
                fox_kt[j * n_pool:(j + 1) * n_pool], fox_vt[j * n_pool:(j + 1) * n_pool],
                fox_lft[j], fox=True, tile=tm)
            outs["fk_s"].append(k.reshape(dec_batch, 1, N_HEADS_FOX, HEAD_DIM))
            outs["fv_s"].append(v.reshape(dec_batch, 1, N_HEADS_FOX, HEAD_DIM))
            outs["fl_s"].append(lft_s[:N_HEADS_FOX].T.reshape(dec_batch, 1, N_HEADS_FOX))
        else:
            lam_init = 0.8 - 0.6 * math.exp(-0.3 * i)
            w_all = w_in_diff[j].astype(BF16)
            lam_vec = jnp.zeros((8, LANES), F32)
            lam_vec = lam_vec.at[0, :HEAD_DIM].set(lam_q1[j]).at[1, :HEAD_DIM].set(lam_k1[j])
            lam_vec = lam_vec.at[2, :HEAD_DIM].set(lam_q2[j]).at[3, :HEAD_DIM].set(lam_k2[j])
            g_sub = g_subln[j].reshape(1, LANES)
            q, k, v, kb, vb, qm = _proj_diff_prompt(h_p, g_pre, w_all, cos_p, sin_p, tm)
            outs["dk_p"].append(jnp.swapaxes(k, 1, 2))
            outs["dv_p"].append(jnp.swapaxes(v, 1, 2))

            q_s, k, v, qm_s = _proj_sample(h_s, g_pre, w_all, (cos_s, sin_s), fox=False)
            o_p, o_s, om_s = _attn(
                q, kb, vb, (lam_vec, g_sub), page_table,
                q_s.reshape(dec_batch, 1, w), k.reshape(dec_batch, 1, w),
                v.reshape(dec_batch, 1, w), None,
                qm_s.reshape(dec_batch, 1, MEM_WIDTH), mem_cache_kt, mem_cache_vt, i,
                diff_k[j * n_pool:(j + 1) * n_pool], diff_v[j * n_pool:(j + 1) * n_pool],
                None, fox=False, tile=tm, lam_init=lam_init)
            outs["dk_s"].append(k.reshape(dec_batch, 1, N_HEADS_DIFF, 2 * HEAD_DIM))
            outs["dv_s"].append(v.reshape(dec_batch, 1, N_HEADS_DIFF, 2 * HEAD_DIM))

        g_post = g_post_mix[i].reshape(1, D_MODEL)
        g_pre_f = g_pre_ffn[i].reshape(1, D_MODEL)
        g_post_f = g_post_ffn[i].reshape(1, D_MODEL)
        h_p = _mix(o_p, (qm, mem_ktb, mem_vtb), h_p, w_out_b, g_post, tm, True, layer=i)
        h_p = _ffn(h_p, g_pre_f, g_post_f, w_gu_b, w_down_b, tm)
        h_s3 = _mix(o_s.reshape(1, dec_batch, w), (om_s.reshape(1, dec_batch, MEM_WIDTH),),
                    h_s.reshape(1, dec_batch, D_MODEL), w_out_b, g_post, dec_batch, False)
        h_s = _ffn(h_s3, g_pre_f, g_post_f, w_gu_b, w_down_b, dec_batch).reshape(
            dec_batch, D_MODEL)

    st = lambda name: jnp.stack(outs[name])
    mem_tokens = lambda t: jnp.transpose(
        t.reshape(depth, batch, MEM_WIDTH // HEAD_DIM, HEAD_DIM, N_MEM), (0, 1, 4, 2, 3))
    return (h_p, h_s.reshape(dec_batch, 1, D_MODEL), st("fk_p"), st("fv_p"), st("fl_p"),
            st("fk_s"), st("fv_s"), st("fl_s"), st("dk_p"), st("dv_p"), st("dk_s"), st("dv_s"),
            mem_tokens(mem_kt), mem_tokens(mem_vt))
```

```python
import functools
import math

import jax
import jax.numpy as jnp
from jax import lax
from jax.experimental import pallas as pl
from jax.experimental.pallas import tpu as pltpu

F32 = jnp.float32
BF16 = jnp.bfloat16

D_MODEL = 1024
HEAD_DIM = 64
SELF_WIDTH = 768
N_HEADS_FOX = 12
N_HEADS_DIFF = 6
N_MEM = 256
MEM_WIDTH = 256
D_FF = 2816
PAGE_SIZE = 128
ROPE_THETA = 10000.0
NORM_EPS = 1e-6
NEG_INF = -1e30
QK_SCALE = HEAD_DIM ** -0.5
LOG2_E = math.log2(math.e)

LANES = 128
HEAD_ROWS = 16
VMEM_LIMIT = 56 * 1024 * 1024

PROMPT_TILE = 512
FF_CHUNK = 1408


def _cparams(n_axes):
    return pltpu.CompilerParams(
        dimension_semantics=("arbitrary",) * n_axes, vmem_limit_bytes=VMEM_LIMIT)


def _rms(x, g):
    ms = jnp.mean(x * x, axis=-1, keepdims=True)
    return x * lax.rsqrt(ms + NORM_EPS) * g


def _split3(x):
    hi = x.astype(BF16)
    r = x - hi.astype(F32)
    mid = r.astype(BF16)
    lo = (r - mid.astype(F32)).astype(BF16)
    return hi, mid, lo


def _dot(a, b):
    return jnp.dot(a, b, preferred_element_type=F32)


def _dot_nt(a, b):
    return lax.dot_general(a, b, (((1,), (1,)), ((), ())), preferred_element_type=F32)


def _dot_exact_right(x, m01):
    n = x.shape[0]
    r = _dot(jnp.concatenate(_split3(x), axis=0), m01)
    return r[0:n] + r[n:2 * n] + r[2 * n:3 * n]


def _log_sigmoid(x):
    return jnp.minimum(x, 0.0) - jnp.log1p(jnp.exp(-jnp.abs(x)))


def _first_half_mask():
    return lax.broadcasted_iota(jnp.int32, (1, LANES), 1) < HEAD_DIM


def _upper_tri(n):
    row = lax.broadcasted_iota(jnp.int32, (n, n), 0)
    col = lax.broadcasted_iota(jnp.int32, (n, n), 1)
    return jnp.where(row <= col, 1.0, 0.0).astype(BF16)


def _head_selector(width):
    r = lax.broadcasted_iota(jnp.int32, (HEAD_ROWS, width), 0)
    f = lax.broadcasted_iota(jnp.int32, (HEAD_ROWS, width), 1)
    return (f // HEAD_DIM) == r


def _proj_fox_prompt_kernel(x_ref, g_ref, wq_ref, wkgt_ref, wvt_ref, wm_ref, bf_ref,
                            q_ref, kt_ref, vt_ref, ktb_ref, vtb_ref, qm_ref, lft_ref, negc_ref,
                            carry_ref, *, tm):
    w = SELF_WIDTH
    xn = _rms(x_ref[0], g_ref[...]).astype(BF16)
    q_ref[0] = (_dot(xn, wq_ref[...]) * (QK_SCALE * LOG2_E)).astype(BF16)
    qm_ref[0] = (_dot(xn, wm_ref[...]) * QK_SCALE).astype(BF16)
    kg = _dot_nt(wkgt_ref[...], xn)
    for t, out_ref, outb_ref in ((kg[0:w], kt_ref, ktb_ref),
                                 (_dot_nt(wvt_ref[...], xn), vt_ref, vtb_ref)):
        out_ref[0] = t
        for p in range(w // LANES):
            outb_ref[0, p, 0] = t[p * LANES:(p + 1) * LANES, :].astype(BF16)

    gate = kg[w:w + HEAD_ROWS] + bf_ref[...]
    head = lax.broadcasted_iota(jnp.int32, (HEAD_ROWS, 1), 0)
    lf = jnp.where(head < N_HEADS_FOX, _log_sigmoid(gate), 0.0)
    lft_ref[0] = lf

    @pl.when(pl.program_id(1) == 0)
    def _():
        carry_ref[...] = jnp.zeros_like(carry_ref)

    cum = _dot_exact_right(lf, _upper_tri(tm)) + carry_ref[:, 0:1]
    carry_ref[...] = jnp.broadcast_to(cum[:, tm - 1:tm], carry_ref.shape)
    negc_ref[0, 0] = cum * (-LOG2_E)


def _proj_fox_prompt(x, g, wq, wkgt, wvt, wm, bf_col, tm):
    b, s, _ = x.shape
    n_pairs = SELF_WIDTH // LANES
    row = lambda i, j: (i, j, 0)
    col = lambda i, j: (i, 0, j)
    const = lambda i, j: (0, 0)
    blocked = lambda i, j: (i, 0, j, 0, 0)
    out_shape = (
        jax.ShapeDtypeStruct((b, s, SELF_WIDTH), BF16),
        jax.ShapeDtypeStruct((b, SELF_WIDTH, s), F32),
        jax.ShapeDtypeStruct((b, SELF_WIDTH, s), F32),
        jax.ShapeDtypeStruct((b, n_pairs, s // tm, LANES, tm), BF16),
        jax.ShapeDtypeStruct((b, n_pairs, s // tm, LANES, tm), BF16),
        jax.ShapeDtypeStruct((b, s, MEM_WIDTH), BF16),
        jax.ShapeDtypeStruct((b, HEAD_ROWS, s), F32),
        jax.ShapeDtypeStruct((b, s // tm, HEAD_ROWS, tm), F32),
    )
    out_specs = (
        pl.BlockSpec((1, tm, SELF_WIDTH), row),
        pl.BlockSpec((1, SELF_WIDTH, tm), col),
        pl.BlockSpec((1, SELF_WIDTH, tm), col),
        pl.BlockSpec((1, n_pairs, 1, LANES, tm), blocked),
        pl.BlockSpec((1, n_pairs, 1, LANES, tm), blocked),
        pl.BlockSpec((1, tm, MEM_WIDTH), row),
        pl.BlockSpec((1, HEAD_ROWS, tm), col),
        pl.BlockSpec((1, 1, HEAD_ROWS, tm), lambda i, j: (i, j, 0, 0)),
    )
    return pl.pallas_call(
        functools.partial(_proj_fox_prompt_kernel, tm=tm),
        grid=(b, s // tm),
        in_specs=[
            pl.BlockSpec((1, tm, D_MODEL), row),
            pl.BlockSpec((1, D_MODEL), const),
            pl.BlockSpec(wq.shape, const),
            pl.BlockSpec(wkgt.shape, const),
            pl.BlockSpec(wvt.shape, const),
            pl.BlockSpec(wm.shape, const),
            pl.BlockSpec(bf_col.shape, const),
        ],
        out_specs=out_specs,
        out_shape=out_shape,
        scratch_shapes=[pltpu.VMEM((HEAD_ROWS, LANES), F32)],
        compiler_params=_cparams(2),
        name="proj_fox_prompt",
    )(x, g, wq, wkgt, wvt, wm, bf_col)


def _rope_chunk(x, cos, sin_signed, first_quarter):
    partner = jnp.where(first_quarter, pltpu.roll(x, 96, 1), pltpu.roll(x, 32, 1))
    return x * cos + partner * sin_signed


def _first_quarter_mask():
    lane = lax.broadcasted_iota(jnp.int32, (1, LANES), 1)
    return (lane % HEAD_DIM) < (HEAD_DIM // 2)


def _proj_diff_prompt_kernel(x_ref, g_ref, w_ref, cos_ref, sin_ref, q_ref, k_ref, v_ref, kb_ref,
                             vb_ref, qm_ref):
    w = SELF_WIDTH
    xn = _rms(x_ref[0], g_ref[...]).astype(BF16)
    cos = cos_ref[...]
    sin_signed = sin_ref[...]
    first_quarter = _first_quarter_mask()
    zq = _dot(xn, w_ref[:, 0:w])
    zk = _dot(xn, w_ref[:, w:2 * w])
    zv = _dot(xn, w_ref[:, 2 * w:3 * w])
    for h in range(N_HEADS_DIFF):
        sl = slice(h * LANES, (h + 1) * LANES)
        qc = _rope_chunk(zq[:, sl], cos, sin_signed, first_quarter)
        q_ref[0, :, sl] = (qc * (QK_SCALE * LOG2_E)).astype(BF16)
        kc = _rope_chunk(zk[:, sl], cos, sin_signed, first_quarter)
        k_ref[0, h] = kc
        kb_ref[0, h] = kc.astype(BF16)
        vc = zv[:, sl]
        v_ref[0, h] = vc
        vb_ref[0, h] = vc.astype(BF16)
    qm_ref[0] = (_dot(xn, w_ref[:, 3 * w:]) * QK_SCALE).astype(BF16)


def _proj_diff_prompt(x, g, w_all, cos_tab, sin_tab, tm):
    b, s, _ = x.shape
    row = lambda i, j: (i, j, 0)
    const = lambda i, j: (0, 0)
    tab = lambda i, j: (j, 0)
    heads = lambda i, j: (i, 0, j, 0)
    kv_f32 = jax.ShapeDtypeStruct((b, N_HEADS_DIFF, s, LANES), F32)
    kv_bf = jax.ShapeDtypeStruct((b, N_HEADS_DIFF, s, LANES), BF16)
    kv_spec = pl.BlockSpec((1, N_HEADS_DIFF, tm, LANES), heads)
    return pl.pallas_call(
        _proj_diff_prompt_kernel,
        grid=(b, s // tm),
        in_specs=[
            pl.BlockSpec((1, tm, D_MODEL), row),
            pl.BlockSpec((1, D_MODEL), const),
            pl.BlockSpec(w_all.shape, const),
            pl.BlockSpec((tm, LANES), tab),
            pl.BlockSpec((tm, LANES), tab),
        ],
        out_specs=(pl.BlockSpec((1, tm, SELF_WIDTH), row), kv_spec, kv_spec, kv_spec, kv_spec,
                   pl.BlockSpec((1, tm, MEM_WIDTH), row)),
        out_shape=(jax.ShapeDtypeStruct((b, s, SELF_WIDTH), BF16), kv_f32, kv_f32, kv_bf, kv_bf,
                   jax.ShapeDtypeStruct((b, s, MEM_WIDTH), BF16)),
        compiler_params=_cparams(2),
        name="proj_diff_prompt",
    )(x, g, w_all, cos_tab, sin_tab)


def _proj_sample_kernel(*refs, fox):
    if fox:
        x_ref, g_ref, w_ref, wgt_ref, bf_ref, q_ref, k_ref, v_ref, qm_ref, lft_ref = refs
    else:
        x_ref, g_ref, w_ref, cos_ref, sin_ref, q_ref, k_ref, v_ref, qm_ref = refs
    w = SELF_WIDTH
    xn = _rms(x_ref[...], g_ref[...]).astype(BF16)
    q = _dot(xn, w_ref[:, 0:w])
    k = _dot(xn, w_ref[:, w:2 * w])
    if not fox:
        first_quarter = _first_quarter_mask()
        rope = lambda t: jnp.concatenate(
            [_rope_chunk(t[:, c * LANES:(c + 1) * LANES], cos_ref[...], sin_ref[...], first_quarter)
             for c in range(w // LANES)], axis=1)
        q, k = rope(q), rope(k)
    q_ref[...] = q * QK_SCALE
    k_ref[...] = k
    v_ref[...] = _dot(xn, w_ref[:, 2 * w:3 * w])
    qm_ref[...] = _dot(xn, w_ref[:, 3 * w:3 * w + MEM_WIDTH]) * QK_SCALE
    if fox:
        gate = _dot_nt(wgt_ref[...], xn) + bf_ref[...]
        head = lax.broadcasted_iota(jnp.int32, (HEAD_ROWS, 1), 0)
        lft_ref[...] = jnp.where(head < N_HEADS_FOX, _log_sigmoid(gate), 0.0)


def _proj_sample(x, g, w_all, extras, fox):
    n = x.shape[0]
    out_shape = [jax.ShapeDtypeStruct((n, SELF_WIDTH), F32)] * 3
    out_shape.append(jax.ShapeDtypeStruct((n, MEM_WIDTH), F32))
    if fox:
        out_shape.append(jax.ShapeDtypeStruct((HEAD_ROWS, n), F32))
    return pl.pallas_call(
        functools.partial(_proj_sample_kernel, fox=fox),
        out_shape=tuple(out_shape),
        compiler_params=pltpu.CompilerParams(vmem_limit_bytes=VMEM_LIMIT),
        name="proj_fox_sample" if fox else "proj_diff_sample",
    )(x, g, w_all, *extras)


def _mem_kv_kernel(mem_ref, g_ref, wt_ref, k_ref, v_ref, kb_ref, vb_ref):
    xn = _rms(mem_ref[0], g_ref[0]).astype(BF16)
    kv = _dot_nt(wt_ref[0], xn)
    k = kv[0:MEM_WIDTH, :]
    v = kv[MEM_WIDTH:, :]
    k_ref[0, 0] = k
    v_ref[0, 0] = v
    kb_ref[0, 0] = k.astype(BF16)
    vb_ref[0, 0] = v.astype(BF16)


def _mem_kv(mem, g_mem, w_kv_t):
    depth = g_mem.shape[0]
    b = mem.shape[0]
    out_block = pl.BlockSpec((1, 1, MEM_WIDTH, N_MEM), lambda i, j: (i, j, 0, 0))
    f32_out = jax.ShapeDtypeStruct((depth, b, MEM_WIDTH, N_MEM), F32)
    bf_out = jax.ShapeDtypeStruct((depth, b, MEM_WIDTH, N_MEM), BF16)
    return pl.pallas_call(
        _mem_kv_kernel,
        grid=(depth, b),
        in_specs=[
            pl.BlockSpec((1, N_MEM, D_MODEL), lambda i, j: (j, 0, 0)),
            pl.BlockSpec((1, 1, D_MODEL), lambda i, j: (i, 0, 0)),
            pl.BlockSpec((1, 2 * MEM_WIDTH, D_MODEL), lambda i, j: (i, 0, 0)),
        ],
        out_specs=(out_block, out_block, out_block, out_block),
        out_shape=(f32_out, f32_out, bf_out, bf_out),
        compiler_params=_cparams(2),
        name="mem_kv",
    )(mem, g_mem.reshape(depth, 1, D_MODEL), w_kv_t)


def _diff_lambda(lam_ref, lam_init):
    t = lam_ref[...]
    a = jnp.sum(t[0:1] * t[1:2], axis=-1, keepdims=True)
    b = jnp.sum(t[2:3] * t[3:4], axis=-1, keepdims=True)
    return jnp.exp(a) - jnp.exp(b) + lam_init


def _head_norm(o, g, lam_init):
    ms = jnp.mean(o * o, axis=-1, keepdims=True)
    return o * lax.rsqrt(ms + NORM_EPS) * g * (1.0 - lam_init)


def _prompt_attention(q_ref, k_ref, v_ref, extra_refs, o_ref, *, fox, tile, lam_init, pair, qi):
    first = _first_half_mask()
    q = q_ref[0]
    zero = jnp.zeros_like(q)
    q_halves = (jnp.where(first, q, zero), jnp.where(first, zero, q))

    def step(j, carry, masked):
        if fox:
            k = k_ref[0, 0, j]
            v = v_ref[0, 0, j]
        else:
            ks = pl.multiple_of(j * tile, tile)
            k = k_ref[0, 0, pl.ds(ks, tile), :]
            v = v_ref[0, 0, pl.ds(ks, tile), :]
        out = []
        for h in range(2):
            m, l, acc = carry[h]
            if fox:
                s = _dot(q_halves[h], k) + extra_refs[0][0, j, pl.ds(2 * pair + h, 1), :]
            else:
                s = _dot_nt(q_halves[h], k)
            if masked:
                row = lax.broadcasted_iota(jnp.int32, (tile, tile), 0)
                col = lax.broadcasted_iota(jnp.int32, (tile, tile), 1)
                s = jnp.where(row >= col, s, NEG_INF)
            m_new = jnp.maximum(m, jnp.max(s, axis=-1, keepdims=True))
            alpha = jnp.exp2(m - m_new)
            p = jnp.exp2(s - m_new)
            l = alpha * l + jnp.sum(p, axis=-1, keepdims=True)
            pv = _dot_nt(p.astype(BF16), v) if fox else _dot(p.astype(BF16), v)
            out.append((m_new, l, alpha * acc + pv))
        return tuple(out)

    init = tuple((jnp.full((tile, 1), NEG_INF, F32), jnp.zeros((tile, 1), F32),
                  jnp.zeros((tile, LANES), F32)) for _ in range(2))
    carry = lax.fori_loop(0, qi, lambda j, c: step(j, c, False), init)

    def diagonal_tile():
        (_, l0, acc0), (_, l1, acc1) = step(qi, carry, True)
        if fox:
            o = jnp.where(first, acc0 / l0, acc1 / l1)
        else:
            lam_ref, g_ref = extra_refs
            lam = _diff_lambda(lam_ref, lam_init)
            o = _head_norm(acc0 / l0 - lam * (acc1 / l1), g_ref[...], lam_init)
        o_ref[0] = o.astype(BF16)

    return diagonal_tile


def _mem_attend_block(qm, mkt, mvt, first):
    zero = jnp.zeros_like(qm)
    outs = []
    for qh in (jnp.where(first, qm, zero), jnp.where(first, zero, qm)):
        s = _dot(qh, mkt)
        p = jnp.exp(s - jnp.max(s, axis=-1, keepdims=True))
        l = jnp.sum(p, axis=-1, keepdims=True)
        outs.append(_dot_nt(p.astype(BF16), mvt) / l)
    return jnp.where(first, outs[0], outs[1])


def _mix_kernel(*refs, with_mem):
    if with_mem:
        o_ref, qm_ref, mk_ref, mv_ref, h_ref, w_ref, g_ref, out_ref = refs
        first = _first_half_mask()
        om = []
        for p in range(MEM_WIDTH // LANES):
            sl = slice(p * LANES, (p + 1) * LANES)
            om.append(_mem_attend_block(qm_ref[0, :, sl], mk_ref[0, 0, sl, :], mv_ref[0, 0, sl, :],
                                        first).astype(BF16))
        o_self = o_ref[0]
    else:
        o_ref, om_ref, h_ref, w_ref, g_ref, out_ref = refs
        o_self = o_ref[0].astype(BF16)
        om = [om_ref[0, :, p * LANES:(p + 1) * LANES].astype(BF16)
              for p in range(MEM_WIDTH // LANES)]
    y = _dot(o_self, w_ref[0:SELF_WIDTH, :])
    for p, om_p in enumerate(om):
        y = y + _dot(om_p, w_ref[SELF_WIDTH + p * LANES:SELF_WIDTH + (p + 1) * LANES, :])
    out_ref[0] = h_ref[0] + _rms(y, g_ref[...])


def _mix(o_self, mem_inputs, h, w_out, g_post, tm, with_mem, layer=0):
    b, s, _ = h.shape
    row = lambda i, j: (i, j, 0)
    const = lambda i, j: (0, 0)
    if with_mem:
        mem_spec = pl.BlockSpec((1, 1, MEM_WIDTH, N_MEM), lambda i, j: (layer, i, 0, 0))
        mem_specs = [pl.BlockSpec((1, tm, MEM_WIDTH), row), mem_spec, mem_spec]
    else:
        mem_specs = [pl.BlockSpec((1, tm, MEM_WIDTH), row)]
    return pl.pallas_call(
        functools.partial(_mix_kernel, with_mem=with_mem),
        grid=(b, s // tm),
        in_specs=[pl.BlockSpec((1, tm, SELF_WIDTH), row)] + mem_specs + [
            pl.BlockSpec((1, tm, D_MODEL), row),
            pl.BlockSpec((D_MODEL, D_MODEL), const),
            pl.BlockSpec((1, D_MODEL), const),
        ],
        out_specs=pl.BlockSpec((1, tm, D_MODEL), row),
        out_shape=jax.ShapeDtypeStruct((b, s, D_MODEL), F32),
        compiler_params=_cparams(2),
        name="mix_prompt" if with_mem else "mix_sample",
    )(o_self, *mem_inputs, h, w_out, g_post)


def _ffn_kernel(h_ref, gpre_ref, gpost_ref, wgu_ref, wd_ref, out_ref):
    h = h_ref[0]
    xn = _rms(h, gpre_ref[...]).astype(BF16)
    f = jnp.zeros(h.shape, F32)
    for c in range(D_FF // FF_CHUNK):
        lo, hi = c * FF_CHUNK, (c + 1) * FF_CHUNK
        gate = _dot(xn, wgu_ref[:, lo:hi])
        up = _dot(xn, wgu_ref[:, D_FF + lo:D_FF + hi])
        a = (gate * jax.nn.sigmoid(gate) * up).astype(BF16)
        f = f + _dot(a, wd_ref[lo:hi, :])
    out_ref[0] = h + _rms(f, gpost_ref[...])


def _ffn(h, g_pre, g_post, w_gu, w_down, tm):
    b, s, _ = h.shape
    row = lambda i, j: (i, j, 0)
    const = lambda i, j: (0, 0)
    return pl.pallas_call(
        _ffn_kernel,
        grid=(b, s // tm),
        in_specs=[
            pl.BlockSpec((1, tm, D_MODEL), row),
            pl.BlockSpec((1, D_MODEL), const),
            pl.BlockSpec((1, D_MODEL), const),
            pl.BlockSpec((D_MODEL, 2 * D_FF), const, pipeline_mode=pl.Buffered(1)),
            pl.BlockSpec((D_FF, D_MODEL), const, pipeline_mode=pl.Buffered(1)),
        ],
        out_specs=pl.BlockSpec((1, tm, D_MODEL), row),
        out_shape=jax.ShapeDtypeStruct((b, s, D_MODEL), F32),
        compiler_params=_cparams(2),
        name="ffn",
    )(h, g_pre, g_post, w_gu, w_down)


def _decode_step(pt_ref, seq, c, q_ref, kcur_ref, vcur_ref, lfcur_ref, qm_ref, mk_ref, mv_ref,
                 lam_ref, g_ref, k_pages, v_pages, lf_ref, o_ref, om_ref, scratch, *, fox,
                 n_chunks, n_pages, lam_init, before_chunk, beside_chunk):
    pages = len(k_pages)
    chunk = pages * PAGE_SIZE
    if fox:
        m_scr, l_scr, acc_scr, carry_scr, lf_scr, p_scr, a_scr = scratch
    else:
        m_scr, l_scr, acc_scr = scratch

    @pl.when(c == 0)
    def _():
        m_scr[...] = jnp.full(m_scr.shape, NEG_INF, F32)
        l_scr[...] = jnp.zeros(l_scr.shape, F32)
        acc_scr[...] = jnp.zeros(acc_scr.shape, F32)
        if fox:
            carry_scr[...] = jnp.zeros(carry_scr.shape, F32)
            lf_scr[...] = jnp.zeros(lf_scr.shape, F32)

    before_chunk()
    sel = _head_selector(SELF_WIDTH)
    q_rows = jnp.where(sel, jnp.broadcast_to(q_ref[0], (HEAD_ROWS, SELF_WIDTH)), 0.0)
    q_blk = q_rows.astype(BF16)

    first_maybe_invalid = n_pages - (n_chunks - 1) * pages
    token = lax.broadcasted_iota(jnp.int32, (1, chunk), 1)
    valid_tokens = token < (n_pages - c * pages) * PAGE_SIZE

    if fox:
        k_cat = jnp.concatenate([kp[0].astype(BF16) for kp in k_pages], axis=1)
        s_all = _dot(q_blk, k_cat)
        lf_parts = []
        for i in range(pages):
            page = pt_ref[seq, jnp.minimum(c * pages + i, n_pages - 1)]
            for h in range(N_HEADS_FOX):
                lf_scr[i * HEAD_ROWS + h:i * HEAD_ROWS + h + 1, :] = lf_ref[h, pl.ds(page, 1), :]
            lf_i = lf_scr[i * HEAD_ROWS:(i + 1) * HEAD_ROWS, :]
            if i >= first_maybe_invalid:
                lf_i = jnp.where(c * pages + i < n_pages, lf_i, 0.0)
            lf_parts.append(lf_i)
        local = _dot_exact_right(jnp.concatenate(lf_parts, axis=0), _upper_tri(PAGE_SIZE))
        carry = carry_scr[:, 0:1]
        cums = []
        for i in range(pages):
            cum = local[i * HEAD_ROWS:(i + 1) * HEAD_ROWS, :] + carry
            carry = cum[:, PAGE_SIZE - 1:PAGE_SIZE]
            cums.append(cum)
        carry_scr[...] = jnp.broadcast_to(carry, carry_scr.shape)
        s_all = s_all - jnp.concatenate(cums, axis=1)
    else:
        s_all = None
        for pr in range(N_HEADS_DIFF // 2):
            k_pair = jnp.concatenate(
                [jnp.concatenate([kp[0, 2 * pr], kp[0, 2 * pr + 1]], axis=1).astype(BF16)
                 for kp in k_pages], axis=0)
            part = _dot_nt(q_blk[:, 2 * pr * LANES:2 * (pr + 1) * LANES], k_pair)
            s_all = part if s_all is None else s_all + part
    if first_maybe_invalid < pages:
        s_all = jnp.where(valid_tokens, s_all, NEG_INF)

    m_old = m_scr[:, 0:1]
    m_new = jnp.maximum(m_old, jnp.max(s_all, axis=1, keepdims=True))
    alpha = jnp.exp(m_old - m_new)
    p_all = jnp.exp(s_all - m_new)
    l_new = alpha * l_scr[:, 0:1] + jnp.sum(p_all, axis=1, keepdims=True)
    m_scr[...] = jnp.broadcast_to(m_new, m_scr.shape)
    l_scr[...] = jnp.broadcast_to(l_new, l_scr.shape)
    if fox:
        p_scr[...] = p_all
        a_scr[...] = jnp.broadcast_to(alpha, a_scr.shape)

        for h in range(N_HEADS_FOX):
            rows = slice(h * HEAD_DIM, (h + 1) * HEAD_DIM)
            acc_h = acc_scr[rows, :] * jnp.broadcast_to(a_scr[h:h + 1, :], (HEAD_DIM, LANES))
            for i in range(pages):
                p_row = p_scr[h:h + 1, i * PAGE_SIZE:(i + 1) * PAGE_SIZE]
                acc_h = acc_h + v_pages[i][0, rows, :] * jnp.broadcast_to(p_row, (HEAD_DIM, LANES))
            acc_scr[rows, :] = acc_h
    else:
        v_cat = jnp.concatenate(
            [jnp.concatenate([vp[0, h] for h in range(N_HEADS_DIFF)], axis=1).astype(BF16)
             for vp in v_pages], axis=0)
        acc_scr[...] = alpha * acc_scr[...] + _dot(p_all.astype(BF16), v_cat)
    beside_chunk()

    @pl.when(c == n_chunks - 1)
    def _():
        s_cur = jnp.sum(q_rows * kcur_ref[0], axis=1, keepdims=True)
        if fox:
            s_cur = s_cur - (carry + lfcur_ref[0])
        m_fin = jnp.maximum(m_new, s_cur)
        a_fin = jnp.exp(m_new - m_fin)
        p_cur = jnp.exp(s_cur - m_fin)
        l_fin = a_fin * l_new + p_cur
        if fox:
            spread = lambda col: jnp.sum(jnp.where(sel, col, 0.0), axis=0, keepdims=True)
            pv_row = jnp.sum(acc_scr[...].T, axis=0, keepdims=True)
            o_ref[0] = (spread(a_fin) * pv_row + spread(p_cur) * vcur_ref[0]) / spread(l_fin)
        else:
            o_rows = (a_fin * acc_scr[...] + p_cur * vcur_ref[0]) / l_fin
            lam = _diff_lambda(lam_ref, lam_init)
            r = lax.broadcasted_iota(jnp.int32, (HEAD_ROWS, SELF_WIDTH), 0)
            f = lax.broadcasted_iota(jnp.int32, (HEAD_ROWS, SELF_WIDTH), 1)
            head2 = 2 * (f // (2 * HEAD_DIM))
            coef = jnp.where(r == head2, 1.0, 0.0) - lam * jnp.where(r == head2 + 1, 1.0, 0.0)
            o = jnp.sum(coef * o_rows, axis=0, keepdims=True)
            for h in range(N_HEADS_DIFF):
                sl = slice(h * LANES, (h + 1) * LANES)
                o_ref[0, :, sl] = _head_norm(o[:, sl], g_ref[...], lam_init)

        sel_m = _head_selector(MEM_WIDTH)
        qm_blk = jnp.where(sel_m, jnp.broadcast_to(qm_ref[0], (HEAD_ROWS, MEM_WIDTH)), 0.0)
        s_m = _dot(qm_blk.astype(BF16), mk_ref[0, 0].astype(BF16))
        p_m = jnp.exp(s_m - jnp.max(s_m, axis=1, keepdims=True))
        l_m = jnp.sum(p_m, axis=1, keepdims=True)
        o_m = _dot_nt(p_m.astype(BF16), mv_ref[0, 0].astype(BF16)) / l_m
        om_ref[0] = jnp.sum(jnp.where(sel_m, o_m, 0.0), axis=0, keepdims=True)


def _attn_kernel(*refs, fox, tile, lam_init, pages, n_chunks, n_pages):
    it = iter(refs)
    pt_ref = next(it)
    q_ref, k_ref, v_ref = next(it), next(it), next(it)
    extra_refs = (next(it),) if fox else (next(it), next(it))
    qs_ref, kcur_ref, vcur_ref = next(it), next(it), next(it)
    lfcur_ref = next(it) if fox else None
    qm_ref, mk_ref, mv_ref = next(it), next(it), next(it)
    k_pages = [next(it) for _ in range(pages)]
    v_pages = [next(it) for _ in range(pages)]
    lf_ref = next(it) if fox else None
    o_ref, os_ref, om_ref = next(it), next(it), next(it)
    scratch = list(it)

    pair, qi = pl.program_id(1), pl.program_id(2)
    step = (pl.program_id(0) * pl.num_programs(1) + pair) * pl.num_programs(2) + qi
    lam_ref, g_ref = (None, None) if fox else extra_refs
    diagonal = []

    def prompt_loop():
        diagonal.append(_prompt_attention(q_ref, k_ref, v_ref, extra_refs, o_ref, fox=fox,
                                          tile=tile, lam_init=lam_init, pair=pair, qi=qi))

    _decode_step(pt_ref, step // n_chunks, step % n_chunks, qs_ref, kcur_ref, vcur_ref, lfcur_ref,
                 qm_ref, mk_ref, mv_ref, lam_ref, g_ref, k_pages, v_pages, lf_ref, os_ref, om_ref,
                 scratch, fox=fox, n_chunks=n_chunks, n_pages=n_pages, lam_init=lam_init,
                 before_chunk=prompt_loop, beside_chunk=lambda: diagonal[0]())


def _attn(q, k, v, extras, page_table, q_s, k_cur, v_cur, lf_cur, qm_s, mem_k, mem_v, layer,
          cache_k, cache_v, cache_lf, fox, tile, lam_init=0.0):
    b, s, w = q.shape
    n_blocks = w // LANES
    nq = s // tile
    db, n_pages = page_table.shape
    n_steps = b * n_blocks * nq
    assert n_steps % db == 0
    n_chunks = n_steps // db
    pages = -(-n_pages // n_chunks)

    def seq_of(i, p, j):
        return ((i * n_blocks + p) * nq + j) // n_chunks

    def page_of(i, p, j, pt, slot):
        step = (i * n_blocks + p) * nq + j
        return pt[step // n_chunks, jnp.minimum((step % n_chunks) * pages + slot, n_pages - 1)]

    qo_spec = pl.BlockSpec((1, tile, LANES), lambda i, p, j, pt: (i, j, p))
    if fox:
        kv_spec = pl.BlockSpec((1, 1, nq, LANES, tile), lambda i, p, j, pt: (i, p, 0, 0, 0))
        extra_specs = [pl.BlockSpec((1, nq, HEAD_ROWS, tile), lambda i, p, j, pt: (i, 0, 0, 0))]
    else:
        kv_spec = pl.BlockSpec((1, 1, s, LANES), lambda i, p, j, pt: (i, p, 0, 0))
        extra_specs = [pl.BlockSpec((8, LANES), lambda i, p, j, pt: (0, 0)),
                       pl.BlockSpec((1, LANES), lambda i, p, j, pt: (0, 0))]
    inputs = [q, k, v, *extras]
    in_specs = [qo_spec, kv_spec, kv_spec] + extra_specs

    seq3 = lambda i, p, j, pt: (seq_of(i, p, j), 0, 0)
    inputs += [q_s, k_cur, v_cur]
    in_specs += [pl.BlockSpec((1, 1, SELF_WIDTH), seq3)] * 3
    if fox:
        inputs.append(lf_cur)
        in_specs.append(pl.BlockSpec((1, HEAD_ROWS, 1), seq3))
    mem_spec = pl.BlockSpec((1, 1, MEM_WIDTH, N_MEM),
                            lambda i, p, j, pt: (layer, seq_of(i, p, j), 0, 0))
    inputs += [qm_s, mem_k, mem_v]
    in_specs += [pl.BlockSpec((1, 1, MEM_WIDTH), seq3), mem_spec, mem_spec]
    if fox:
        page_specs = [pl.BlockSpec((1, SELF_WIDTH, PAGE_SIZE),
                                   lambda i, p, j, pt, t=t: (page_of(i, p, j, pt, t), 0, 0))
                      for t in range(pages)]
    else:
        page_specs = [pl.BlockSpec((1, N_HEADS_DIFF, PAGE_SIZE, LANES),
                                   lambda i, p, j, pt, t=t: (page_of(i, p, j, pt, t), 0, 0, 0))
                      for t in range(pages)]
    inputs += [cache_k] * pages + [cache_v] * pages
    in_specs += page_specs * 2
    if fox:
        inputs.append(cache_lf)
        in_specs.append(pl.BlockSpec(cache_lf.shape, lambda i, p, j, pt: (0, 0, 0),
                                     pipeline_mode=pl.Buffered(1)))
    stat = pltpu.VMEM((HEAD_ROWS, LANES), F32)
    if fox:
        scratch = [stat, stat, pltpu.VMEM((SELF_WIDTH, LANES), F32),
                   stat, pltpu.VMEM((pages * HEAD_ROWS, PAGE_SIZE), F32),
                   pltpu.VMEM((HEAD_ROWS, pages * PAGE_SIZE), F32), stat]
    else:
        scratch = [stat, stat, pltpu.VMEM((HEAD_ROWS, SELF_WIDTH), F32)]
    grid_spec = pltpu.PrefetchScalarGridSpec(
        num_scalar_prefetch=1,
        grid=(b, n_blocks, nq),
        in_specs=in_specs,
        out_specs=(qo_spec, pl.BlockSpec((1, 1, SELF_WIDTH), seq3),
                   pl.BlockSpec((1, 1, MEM_WIDTH), seq3)),
        scratch_shapes=scratch,
    )
    return pl.pallas_call(
        functools.partial(_attn_kernel, fox=fox, tile=tile, lam_init=lam_init, pages=pages,
                          n_chunks=n_chunks, n_pages=n_pages),
        grid_spec=grid_spec,
        out_shape=(jax.ShapeDtypeStruct((b, s, w), BF16),
                   jax.ShapeDtypeStruct((db, 1, SELF_WIDTH), F32),
                   jax.ShapeDtypeStruct((db, 1, MEM_WIDTH), F32)),
        compiler_params=_cparams(3),
        name="attn_fox" if fox else "attn_diff",
    )(page_table, *inputs)


def _rope_tables(pos):
    inv = ROPE_THETA ** (-jnp.arange(0, HEAD_DIM, 2, dtype=F32) / HEAD_DIM)
    ang = pos.astype(F32)[:, None] * inv[None, :]
    cos = jnp.cos(ang)
    sin = jnp.sin(ang)
    return jnp.tile(cos, (1, 4)), jnp.tile(jnp.concatenate([-sin, sin], axis=-1), (1, 2))


def kernel(x_prompt, x_sample, cache_fox_k, cache_fox_v, cache_fox_logf, cache_diff_k, cache_diff_v, cache_mem_k, cache_mem_v, page_table, mem_prompt, w_in_fox, b_f_fox, w_in_diff, lam_q1, lam_k1, lam_q2, lam_k2, g_subln, g_pre_mix, g_post_mix, g_pre_ffn, g_post_ffn, g_mem, w_mem_kv, w_out, w_gate_up, w_down):
    depth = g_pre_mix.shape[0]
    batch, seq, _ = x_prompt.shape
    dec_batch, dec_seq, _ = x_sample.shape
    assert dec_seq == 1
    n_pool = cache_fox_k.shape[1]
    past_len = page_table.shape[1] * PAGE_SIZE
    w = SELF_WIDTH
    tm = PROMPT_TILE

    h_p = x_prompt
    h_s = x_sample.reshape(dec_batch, D_MODEL)

    mem_kt, mem_vt, mem_ktb, mem_vtb = _mem_kv(
        mem_prompt, g_mem, jnp.swapaxes(w_mem_kv, 1, 2).astype(BF16))
    mem_cache_kt = jnp.transpose(cache_mem_k, (0, 1, 3, 4, 2)).reshape(
        depth, dec_batch, MEM_WIDTH, N_MEM)
    mem_cache_vt = jnp.transpose(cache_mem_v, (0, 1, 3, 4, 2)).reshape(
        depth, dec_batch, MEM_WIDTH, N_MEM)

    fox_kt = jnp.transpose(cache_fox_k, (0, 1, 3, 4, 2)).reshape(-1, w, PAGE_SIZE)
    fox_vt = jnp.transpose(cache_fox_v, (0, 1, 3, 4, 2)).reshape(-1, w, PAGE_SIZE)
    fox_lft = jnp.transpose(cache_fox_logf, (0, 3, 1, 2))
    diff_k = jnp.transpose(cache_diff_k, (0, 1, 3, 2, 4)).reshape(-1, N_HEADS_DIFF, PAGE_SIZE, LANES)
    diff_v = jnp.transpose(cache_diff_v, (0, 1, 3, 2, 4)).reshape(-1, N_HEADS_DIFF, PAGE_SIZE, LANES)

    cos_p, sin_p = _rope_tables(jnp.arange(seq))
    cos_s, sin_s = _rope_tables(past_len + jnp.zeros((dec_batch,), jnp.int32))

    outs = {name: [] for name in ("fk_p", "fv_p", "fl_p", "fk_s", "fv_s", "fl_s",
                                  "dk_p", "dv_p", "dk_s", "dv_s")}
    for i in range(depth):
        j = i // 2
        g_pre = g_pre_mix[i].reshape(1, D_MODEL)
        w_out_b = w_out[i].astype(BF16)
        w_gu_b = w_gate_up[i].astype(BF16)
        w_down_b = w_down[i].astype(BF16)
        if i % 2 == 0:
            w_in = w_in_fox[j]
            w_in_t = w_in.T
            gate_lo, gate_hi = 3 * w, 3 * w + N_HEADS_FOX
            wgt = jnp.zeros((HEAD_ROWS, D_MODEL), F32).at[:N_HEADS_FOX].set(
                w_in_t[gate_lo:gate_hi]).astype(BF16)
            bf_col = jnp.zeros((HEAD_ROWS, 1), F32).at[:N_HEADS_FOX, 0].set(b_f_fox[j])
            w_nat = jnp.concatenate([w_in[:, :gate_lo], w_in[:, gate_hi:]], axis=1).astype(BF16)
            wkgt = jnp.concatenate([w_in_t[w:2 * w].astype(BF16), wgt], axis=0)
            q, kt, vt, ktb, vtb, qm, lft, negc = _proj_fox_prompt(
                h_p, g_pre, w_nat[:, :w], wkgt, w_in_t[2 * w:3 * w].astype(BF16),
                w_nat[:, 3 * w:], bf_col, tm)
            to_tokens = lambda t: jnp.transpose(
                t.reshape(batch, N_HEADS_FOX, HEAD_DIM, seq), (0, 3, 1, 2))
            outs["fk_p"].append(to_tokens(kt))
            outs["fv_p"].append(to_tokens(vt))
            outs["fl_p"].append(jnp.swapaxes(lft[:, :N_HEADS_FOX], 1, 2))

            q_s, k, v, qm_s, lft_s = _proj_sample(h_s, g_pre, w_nat, (wgt, bf_col), fox=True)
            o_p, o_s, om_s = _attn(
                q, ktb, vtb, (negc,), page_table,
                q_s.reshape(dec_batch, 1, w), k.reshape(dec_batch, 1, w),
                v.reshape(dec_batch, 1, w), lft_s.T.reshape(dec_batch, HEAD_ROWS, 1),
                qm_s.reshape(dec_batch, 1, MEM_WIDTH), mem_cache_kt, mem_cache_vt, i,
                fox_kt[j * n_pool:(j + 1) * n_pool], fox_vt[j * n_pool:(j + 1) * n_pool],
                fox_lft[j], fox=True, tile=tm)
            outs["fk_s"].append(k.reshape(dec_batch, 1, N_HEADS_FOX, HEAD_DIM))
            outs["fv_s"].append(v.reshape(dec_batch, 1, N_HEADS_FOX, HEAD_DIM))
            outs["fl_s"].append(lft_s[:N_HEADS_FOX].T.reshape(dec_batch, 1, N_HEADS_FOX))
        else:
            lam_init = 0.8 - 0.6 * math.exp(-0.3 * i)
            w_all = w_in_diff[j].astype(BF16)
            lam_vec = jnp.zeros((8, LANES), F32)
            lam_vec = lam_vec.at[0, :HEAD_DIM].set(lam_q1[j]).at[1, :HEAD_DIM].set(lam_k1[j])
            lam_vec = lam_vec.at[2, :HEAD_DIM].set(lam_q2[j]).at[3, :HEAD_DIM].set(lam_k2[j])
            g_sub = g_subln[j].reshape(1, LANES)
            q, k, v, kb, vb, qm = _proj_diff_prompt(h_p, g_pre, w_all, cos_p, sin_p, tm)
            outs["dk_p"].append(jnp.swapaxes(k, 1, 2))
            outs["dv_p"].append(jnp.swapaxes(v, 1, 2))

            q_s, k, v, qm_s = _proj_sample(h_s, g_pre, w_all, (cos_s, sin_s), fox=False)
            o_p, o_s, om_s = _attn(
                q, kb, vb, (lam_vec, g_sub), page_table,
                q_s.reshape(dec_batch, 1, w), k.reshape(dec_batch, 1, w),
                v.reshape(dec_batch, 1, w), None,
                qm_s.reshape(dec_batch, 1, MEM_WIDTH), mem_cache_kt, mem_cache_vt, i,
                diff_k[j * n_pool:(j + 1) * n_pool], diff_v[j * n_pool:(j + 1) * n_pool],
                None, fox=False, tile=tm, lam_init=lam_init)
            outs["dk_s"].append(k.reshape(dec_batch, 1, N_HEADS_DIFF, 2 * HEAD_DIM))
            outs["dv_s"].append(v.reshape(dec_batch, 1, N_HEADS_DIFF, 2 * HEAD_DIM))

        g_post = g_post_mix[i].reshape(1, D_MODEL)
        g_pre_f = g_pre_ffn[i].reshape(1, D_MODEL)
        g_post_f = g_post_ffn[i].reshape(1, D_MODEL)
        h_p = _mix(o_p, (qm, mem_ktb, mem_vtb), h_p, w_out_b, g_post, tm, True, layer=i)
        h_p = _ffn(h_p, g_pre_f, g_post_f, w_gu_b, w_down_b, tm)
        h_s3 = _mix(o_s.reshape(1, dec_batch, w), (om_s.reshape(1, dec_batch, MEM_WIDTH),),
                    h_s.reshape(1, dec_batch, D_MODEL), w_out_b, g_post, dec_batch, False)
        h_s = _ffn(h_s3, g_pre_f, g_post_f, w_gu_b, w_down_b, dec_batch).reshape(
            dec_batch, D_MODEL)

    st = lambda name: jnp.stack(outs[name])
    mem_tokens = lambda t: jnp.transpose(
        t.reshape(depth, batch, MEM_WIDTH // HEAD_DIM, HEAD_DIM, N_MEM), (0, 1, 4, 2, 3))
    return (h_p, h_s.reshape(dec_batch, 1, D_MODEL), st("fk_p"), st("fv_p"), st("fl_p"),
            st("fk_s"), st("fv_s"), st("fl_s"), st("dk_p"), st("dv_p"), st("dk_s"), st("dv_s"),
            mem_tokens(mem_kt), mem_tokens(mem_vt))
```

```python
import functools
import math

import jax
import jax.numpy as jnp
from jax import lax
from jax.experimental import pallas as pl
from jax.experimental.pallas import tpu as pltpu

F32 = jnp.float32
BF16 = jnp.bfloat16

D_MODEL = 1024
HEAD_DIM = 64
SELF_WIDTH = 768
N_HEADS_FOX = 12
N_HEADS_DIFF = 6
N_MEM = 256
MEM_WIDTH = 256
D_FF = 2816
PAGE_SIZE = 128
ROPE_THETA = 10000.0
NORM_EPS = 1e-6
NEG_INF = -1e30
QK_SCALE = HEAD_DIM ** -0.5
LOG2_E = math.log2(math.e)

LANES = 128
HEAD_ROWS = 16
VMEM_LIMIT = 56 * 1024 * 1024

PROMPT_TILE = 512
FF_CHUNK = 1408


def _cparams(n_axes):
    return pltpu.CompilerParams(
        dimension_semantics=("arbitrary",) * n_axes, vmem_limit_bytes=VMEM_LIMIT)


def _rms(x, g):
    ms = jnp.mean(x * x, axis=-1, keepdims=True)
    return x * lax.rsqrt(ms + NORM_EPS) * g


def _split3(x):
    hi = x.astype(BF16)
    r = x - hi.astype(F32)
    mid = r.astype(BF16)
    lo = (r - mid.astype(F32)).astype(BF16)
    return hi, mid, lo


def _dot(a, b):
    return jnp.dot(a, b, preferred_element_type=F32)


def _dot_nt(a, b):
    return lax.dot_general(a, b, (((1,), (1,)), ((), ())), preferred_element_type=F32)


def _dot_exact_right(x, m01):
    n = x.shape[0]
    r = _dot(jnp.concatenate(_split3(x), axis=0), m01)
    return r[0:n] + r[n:2 * n] + r[2 * n:3 * n]


def _log_sigmoid(x):
    return jnp.minimum(x, 0.0) - jnp.log1p(jnp.exp(-jnp.abs(x)))


def _first_half_mask():
    return lax.broadcasted_iota(jnp.int32, (1, LANES), 1) < HEAD_DIM


def _upper_tri(n):
    row = lax.broadcasted_iota(jnp.int32, (n, n), 0)
    col = lax.broadcasted_iota(jnp.int32, (n, n), 1)
    return jnp.where(row <= col, 1.0, 0.0).astype(BF16)


def _head_selector(width):
    r = lax.broadcasted_iota(jnp.int32, (HEAD_ROWS, width), 0)
    f = lax.broadcasted_iota(jnp.int32, (HEAD_ROWS, width), 1)
    return (f // HEAD_DIM) == r


def _proj_fox_prompt_kernel(x_ref, g_ref, wqt_ref, wk_ref, wkgt_ref, wvt_ref, wm_ref, bf_ref,
                            qt_ref, kaug_ref, kt_ref, vt_ref, vtb_ref, qm_ref, lft_ref,
                            carry_ref, *, tm):
    w = SELF_WIDTH
    n_pairs = w // LANES
    xn = _rms(x_ref[0], g_ref[...]).astype(BF16)
    qm_ref[0] = (_dot(xn, wm_ref[...]) * QK_SCALE).astype(BF16)
    qt = _dot_nt(wqt_ref[...], xn) * (QK_SCALE * LOG2_E)
    kg = _dot_nt(wkgt_ref[...], xn)
    vt = _dot_nt(wvt_ref[...], xn)
    kt_ref[0] = kg[0:w]
    vt_ref[0] = vt
    for p in range(n_pairs):
        rows = slice(p * LANES, (p + 1) * LANES)
        qt_ref[0, p] = qt[rows, :].astype(BF16)
        vtb_ref[0, p, 0] = vt[rows, :].astype(BF16)

    gate = kg[w:w + HEAD_ROWS] + bf_ref[...]
    head = lax.broadcasted_iota(jnp.int32, (HEAD_ROWS, 1), 0)
    lf = jnp.where(head < N_HEADS_FOX, _log_sigmoid(gate), 0.0)
    lft_ref[0] = lf

    @pl.when(pl.program_id(1) == 0)
    def _():
        carry_ref[...] = jnp.zeros_like(carry_ref)

    cum = _dot_exact_right(lf, _upper_tri(tm)) + carry_ref[:, 0:1]
    carry_ref[...] = jnp.broadcast_to(cum[:, tm - 1:tm], carry_ref.shape)

    hi, mid, lo = _split3(cum * (-LOG2_E))
    parts = jnp.concatenate([hi.astype(F32), mid.astype(F32), lo.astype(F32),
                             jnp.zeros((LANES - 3 * HEAD_ROWS, tm), F32)], axis=0)
    bias_lanes = parts.T.astype(BF16)
    kn = _dot(xn, wk_ref[...])
    for p in range(n_pairs):
        kaug_ref[0, p, :, 0:LANES] = kn[:, p * LANES:(p + 1) * LANES].astype(BF16)
        kaug_ref[0, p, :, LANES:2 * LANES] = bias_lanes


def _proj_fox_prompt(x, g, wqt, wk, wkgt, wvt, wm, bf_col, tm):
    b, s, _ = x.shape
    n_pairs = SELF_WIDTH // LANES
    row = lambda i, j: (i, j, 0)
    col = lambda i, j: (i, 0, j)
    const = lambda i, j: (0, 0)
    out_shape = (
        jax.ShapeDtypeStruct((b, n_pairs, LANES, s), BF16),
        jax.ShapeDtypeStruct((b, n_pairs, s, 2 * LANES), BF16),
        jax.ShapeDtypeStruct((b, SELF_WIDTH, s), F32),
        jax.ShapeDtypeStruct((b, SELF_WIDTH, s), F32),
        jax.ShapeDtypeStruct((b, n_pairs, s // tm, LANES, tm), BF16),
        jax.ShapeDtypeStruct((b, s, MEM_WIDTH), BF16),
        jax.ShapeDtypeStruct((b, HEAD_ROWS, s), F32),
    )
    out_specs = (
        pl.BlockSpec((1, n_pairs, LANES, tm), lambda i, j: (i, 0, 0, j)),
        pl.BlockSpec((1, n_pairs, tm, 2 * LANES), lambda i, j: (i, 0, j, 0)),
        pl.BlockSpec((1, SELF_WIDTH, tm), col),
        pl.BlockSpec((1, SELF_WIDTH, tm), col),
        pl.BlockSpec((1, n_pairs, 1, LANES, tm), lambda i, j: (i, 0, j, 0, 0)),
        pl.BlockSpec((1, tm, MEM_WIDTH), row),
        pl.BlockSpec((1, HEAD_ROWS, tm), col),
    )
    weights = (wqt, wk, wkgt, wvt, wm, bf_col)
    return pl.pallas_call(
        functools.partial(_proj_fox_prompt_kernel, tm=tm),
        grid=(b, s // tm),
        in_specs=[pl.BlockSpec((1, tm, D_MODEL), row), pl.BlockSpec((1, D_MODEL), const)]
        + [pl.BlockSpec(a.shape, const) for a in weights],
        out_specs=out_specs,
        out_shape=out_shape,
        scratch_shapes=[pltpu.VMEM((HEAD_ROWS, LANES), F32)],
        compiler_params=_cparams(2),
        name="proj_fox_prompt",
    )(x, g, *weights)


def _rope_chunk(x, cos, sin_signed, first_quarter):
    partner = jnp.where(first_quarter, pltpu.roll(x, 96, 1), pltpu.roll(x, 32, 1))
    return x * cos + partner * sin_signed


def _first_quarter_mask():
    lane = lax.broadcasted_iota(jnp.int32, (1, LANES), 1)
    return (lane % HEAD_DIM) < (HEAD_DIM // 2)


def _proj_diff_prompt_kernel(x_ref, g_ref, w_ref, cos_ref, sin_ref, q_ref, k_ref, v_ref, kb_ref,
                             vb_ref, qm_ref):
    w = SELF_WIDTH
    xn = _rms(x_ref[0], g_ref[...]).astype(BF16)
    cos = cos_ref[...]
    sin_signed = sin_ref[...]
    first_quarter = _first_quarter_mask()
    zq = _dot(xn, w_ref[:, 0:w])
    zk = _dot(xn, w_ref[:, w:2 * w])
    zv = _dot(xn, w_ref[:, 2 * w:3 * w])
    for h in range(N_HEADS_DIFF):
        sl = slice(h * LANES, (h + 1) * LANES)
        qc = _rope_chunk(zq[:, sl], cos, sin_signed, first_quarter)
        q_ref[0, :, sl] = (qc * (QK_SCALE * LOG2_E)).astype(BF16)
        kc = _rope_chunk(zk[:, sl], cos, sin_signed, first_quarter)
        k_ref[0, h] = kc
        kb_ref[0, h] = kc.astype(BF16)
        vc = zv[:, sl]
        v_ref[0, h] = vc
        vb_ref[0, h] = vc.astype(BF16)
    qm_ref[0] = (_dot(xn, w_ref[:, 3 * w:]) * QK_SCALE).astype(BF16)


def _proj_diff_prompt(x, g, w_all, cos_tab, sin_tab, tm):
    b, s, _ = x.shape
    row = lambda i, j: (i, j, 0)
    const = lambda i, j: (0, 0)
    tab = lambda i, j: (j, 0)
    heads = lambda i, j: (i, 0, j, 0)
    kv_f32 = jax.ShapeDtypeStruct((b, N_HEADS_DIFF, s, LANES), F32)
    kv_bf = jax.ShapeDtypeStruct((b, N_HEADS_DIFF, s, LANES), BF16)
    kv_spec = pl.BlockSpec((1, N_HEADS_DIFF, tm, LANES), heads)
    return pl.pallas_call(
        _proj_diff_prompt_kernel,
        grid=(b, s // tm),
        in_specs=[
            pl.BlockSpec((1, tm, D_MODEL), row),
            pl.BlockSpec((1, D_MODEL), const),
            pl.BlockSpec(w_all.shape, const),
            pl.BlockSpec((tm, LANES), tab),
            pl.BlockSpec((tm, LANES), tab),
        ],
        out_specs=(pl.BlockSpec((1, tm, SELF_WIDTH), row), kv_spec, kv_spec, kv_spec, kv_spec,
                   pl.BlockSpec((1, tm, MEM_WIDTH), row)),
        out_shape=(jax.ShapeDtypeStruct((b, s, SELF_WIDTH), BF16), kv_f32, kv_f32, kv_bf, kv_bf,
                   jax.ShapeDtypeStruct((b, s, MEM_WIDTH), BF16)),
        compiler_params=_cparams(2),
        name="proj_diff_prompt",
    )(x, g, w_all, cos_tab, sin_tab)


def _proj_sample_kernel(*refs, fox):
    if fox:
        x_ref, g_ref, w_ref, wgt_ref, bf_ref, q_ref, k_ref, v_ref, qm_ref, lft_ref = refs
    else:
        x_ref, g_ref, w_ref, cos_ref, sin_ref, q_ref, k_ref, v_ref, qm_ref = refs
    w = SELF_WIDTH
    xn = _rms(x_ref[...], g_ref[...]).astype(BF16)
    q = _dot(xn, w_ref[:, 0:w])
    k = _dot(xn, w_ref[:, w:2 * w])
    if not fox:
        first_quarter = _first_quarter_mask()
        rope = lambda t: jnp.concatenate(
            [_rope_chunk(t[:, c * LANES:(c + 1) * LANES], cos_ref[...], sin_ref[...], first_quarter)
             for c in range(w // LANES)], axis=1)
        q, k = rope(q), rope(k)
    q_ref[...] = q * QK_SCALE
    k_ref[...] = k
    v_ref[...] = _dot(xn, w_ref[:, 2 * w:3 * w])
    qm_ref[...] = _dot(xn, w_ref[:, 3 * w:3 * w + MEM_WIDTH]) * QK_SCALE
    if fox:
        gate = _dot_nt(wgt_ref[...], xn) + bf_ref[...]
        head = lax.broadcasted_iota(jnp.int32, (HEAD_ROWS, 1), 0)
        lft_ref[...] = jnp.where(head < N_HEADS_FOX, _log_sigmoid(gate), 0.0)


def _proj_sample(x, g, w_all, extras, fox):
    n = x.shape[0]
    out_shape = [jax.ShapeDtypeStruct((n, SELF_WIDTH), F32)] * 3
    out_shape.append(jax.ShapeDtypeStruct((n, MEM_WIDTH), F32))
    if fox:
        out_shape.append(jax.ShapeDtypeStruct((HEAD_ROWS, n), F32))
    return pl.pallas_call(
        functools.partial(_proj_sample_kernel, fox=fox),
        out_shape=tuple(out_shape),
        compiler_params=pltpu.CompilerParams(vmem_limit_bytes=VMEM_LIMIT),
        name="proj_fox_sample" if fox else "proj_diff_sample",
    )(x, g, w_all, *extras)


def _mem_kv_kernel(mem_ref, g_ref, wt_ref, k_ref, v_ref, kb_ref, vb_ref):
    xn = _rms(mem_ref[0], g_ref[0]).astype(BF16)
    kv = _dot_nt(wt_ref[0], xn)
    k = kv[0:MEM_WIDTH, :]
    v = kv[MEM_WIDTH:, :]
    k_ref[0, 0] = k
    v_ref[0, 0] = v
    kb_ref[0, 0] = k.astype(BF16)
    vb_ref[0, 0] = v.astype(BF16)


def _mem_kv(mem, g_mem, w_kv_t):
    depth = g_mem.shape[0]
    b = mem.shape[0]
    out_block = pl.BlockSpec((1, 1, MEM_WIDTH, N_MEM), lambda i, j: (i, j, 0, 0))
    f32_out = jax.ShapeDtypeStruct((depth, b, MEM_WIDTH, N_MEM), F32)
    bf_out = jax.ShapeDtypeStruct((depth, b, MEM_WIDTH, N_MEM), BF16)
    return pl.pallas_call(
        _mem_kv_kernel,
        grid=(depth, b),
        in_specs=[
            pl.BlockSpec((1, N_MEM, D_MODEL), lambda i, j: (j, 0, 0)),
            pl.BlockSpec((1, 1, D_MODEL), lambda i, j: (i, 0, 0)),
            pl.BlockSpec((1, 2 * MEM_WIDTH, D_MODEL), lambda i, j: (i, 0, 0)),
        ],
        out_specs=(out_block, out_block, out_block, out_block),
        out_shape=(f32_out, f32_out, bf_out, bf_out),
        compiler_params=_cparams(2),
        name="mem_kv",
    )(mem, g_mem.reshape(depth, 1, D_MODEL), w_kv_t)


def _diff_lambda(lam_ref, lam_init):
    t = lam_ref[...]
    a = jnp.sum(t[0:1] * t[1:2], axis=-1, keepdims=True)
    b = jnp.sum(t[2:3] * t[3:4], axis=-1, keepdims=True)
    return jnp.exp(a) - jnp.exp(b) + lam_init


def _head_norm(o, g, lam_init):
    ms = jnp.mean(o * o, axis=-1, keepdims=True)
    return o * lax.rsqrt(ms + NORM_EPS) * g * (1.0 - lam_init)


def _prompt_attention(q_ref, k_ref, v_ref, extra_refs, o_ref, *, tile, lam_init, qi):
    first = _first_half_mask()
    q = q_ref[0]
    zero = jnp.zeros_like(q)
    q_halves = (jnp.where(first, q, zero), jnp.where(first, zero, q))

    def step(j, carry, masked):
        ks = pl.multiple_of(j * tile, tile)
        k = k_ref[0, 0, pl.ds(ks, tile), :]
        v = v_ref[0, 0, pl.ds(ks, tile), :]
        scores = [_dot_nt(q_halves[h], k) for h in range(2)]
        out = []
        for h in range(2):
            m, l, acc = carry[h]
            s = scores[h]
            if masked:
                row = lax.broadcasted_iota(jnp.int32, (tile, tile), 0)
                col = lax.broadcasted_iota(jnp.int32, (tile, tile), 1)
                s = jnp.where(row >= col, s, NEG_INF)
            m_new = jnp.maximum(m, jnp.max(s, axis=-1, keepdims=True))
            alpha = jnp.exp2(m - m_new)
            p = jnp.exp2(s - m_new)
            l = alpha * l + jnp.sum(p, axis=-1, keepdims=True)
            out.append((m_new, l, alpha * acc + _dot(p.astype(BF16), v)))
        return tuple(out)

    init = tuple((jnp.full((tile, 1), NEG_INF, F32), jnp.zeros((tile, 1), F32),
                  jnp.zeros((tile, LANES), F32)) for _ in range(2))
    carry = lax.fori_loop(0, qi, lambda j, c: step(j, c, False), init)

    def diagonal_tile():
        (_, l0, acc0), (_, l1, acc1) = step(qi, carry, True)
        lam_ref, g_ref = extra_refs
        lam = _diff_lambda(lam_ref, lam_init)
        o = _head_norm(acc0 / l0 - lam * (acc1 / l1), g_ref[...], lam_init)
        o_ref[0] = o.astype(BF16)

    return diagonal_tile


def _reduce_rows(x, op):
    while x.shape[0] > 8:
        half = x.shape[0] // 2
        x = op(x[:half], x[half:])
    return jnp.max(x, axis=0, keepdims=True) if op is jnp.maximum else jnp.sum(
        x, axis=0, keepdims=True)


def _prompt_attention_fox(qt_ref, k_ref, v_ref, o_ref, *, tile, pair, qi):
    qt = qt_ref[0, 0]
    feat = lax.broadcasted_iota(jnp.int32, (LANES, 1), 0)
    zero = jnp.zeros_like(qt)
    weights = []
    for h in range(2):
        head = 2 * pair + h
        ones_rows = (feat == head) | (feat == HEAD_ROWS + head) | (feat == 2 * HEAD_ROWS + head)
        ones = jnp.where(ones_rows, 1.0, 0.0).astype(BF16)
        q_h = jnp.where((feat // HEAD_DIM) == h, qt, zero)
        weights.append(jnp.concatenate([q_h, jnp.broadcast_to(ones, qt.shape)], axis=0))

    def step(j, carry, masked):
        ks = pl.multiple_of(j * tile, tile)
        k = k_ref[0, 0, pl.ds(ks, tile), :]
        v = v_ref[0, 0, j]
        scores = [_dot(k, weights[h]) for h in range(2)]
        out = []
        for h in range(2):
            m, l, acc = carry[h]
            s = scores[h]
            if masked:
                key = lax.broadcasted_iota(jnp.int32, (tile, tile), 0)
                qry = lax.broadcasted_iota(jnp.int32, (tile, tile), 1)
                s = jnp.where(key <= qry, s, NEG_INF)
            m_new = jnp.maximum(m, _reduce_rows(s, jnp.maximum))
            alpha = jnp.exp2(m - m_new)
            p = jnp.exp2(s - m_new)
            l = alpha * l + _reduce_rows(p, jnp.add)
            out.append((m_new, l, alpha * acc + _dot(v, p.astype(BF16))))
        return tuple(out)

    init = tuple((jnp.full((1, tile), NEG_INF, F32), jnp.zeros((1, tile), F32),
                  jnp.zeros((LANES, tile), F32)) for _ in range(2))
    carry = lax.fori_loop(0, qi, lambda j, c: step(j, c, False), init)

    def diagonal_tile():
        (_, l0, acc0), (_, l1, acc1) = step(qi, carry, True)
        o_t = jnp.where(feat < HEAD_DIM, acc0 / l0, acc1 / l1)
        o_ref[0] = o_t.T.astype(BF16)

    return diagonal_tile


def _mem_attend_block(qm, mkt, mvt, first):
    zero = jnp.zeros_like(qm)
    outs = []
    for qh in (jnp.where(first, qm, zero), jnp.where(first, zero, qm)):
        s = _dot(qh, mkt)
        p = jnp.exp(s - jnp.max(s, axis=-1, keepdims=True))
        l = jnp.sum(p, axis=-1, keepdims=True)
        outs.append(_dot_nt(p.astype(BF16), mvt) / l)
    return jnp.where(first, outs[0], outs[1])


def _mix_kernel(*refs, with_mem):
    if with_mem:
        o_ref, qm_ref, mk_ref, mv_ref, h_ref, w_ref, g_ref, out_ref = refs
        first = _first_half_mask()
        om = []
        for p in range(MEM_WIDTH // LANES):
            sl = slice(p * LANES, (p + 1) * LANES)
            om.append(_mem_attend_block(qm_ref[0, :, sl], mk_ref[0, 0, sl, :], mv_ref[0, 0, sl, :],
                                        first).astype(BF16))
        o_self = o_ref[0]
    else:
        o_ref, om_ref, h_ref, w_ref, g_ref, out_ref = refs
        o_self = o_ref[0].astype(BF16)
        om = [om_ref[0, :, p * LANES:(p + 1) * LANES].astype(BF16)
              for p in range(MEM_WIDTH // LANES)]
    y = _dot(o_self, w_ref[0:SELF_WIDTH, :])
    for p, om_p in enumerate(om):
        y = y + _dot(om_p, w_ref[SELF_WIDTH + p * LANES:SELF_WIDTH + (p + 1) * LANES, :])
    out_ref[0] = h_ref[0] + _rms(y, g_ref[...])


def _mix(o_self, mem_inputs, h, w_out, g_post, tm, with_mem, layer=0):
    b, s, _ = h.shape
    row = lambda i, j: (i, j, 0)
    const = lambda i, j: (0, 0)
    if with_mem:
        mem_spec = pl.BlockSpec((1, 1, MEM_WIDTH, N_MEM), lambda i, j: (layer, i, 0, 0))
        mem_specs = [pl.BlockSpec((1, tm, MEM_WIDTH), row), mem_spec, mem_spec]
    else:
        mem_specs = [pl.BlockSpec((1, tm, MEM_WIDTH), row)]
    return pl.pallas_call(
        functools.partial(_mix_kernel, with_mem=with_mem),
        grid=(b, s // tm),
        in_specs=[pl.BlockSpec((1, tm, SELF_WIDTH), row)] + mem_specs + [
            pl.BlockSpec((1, tm, D_MODEL), row),
            pl.BlockSpec((D_MODEL, D_MODEL), const),
            pl.BlockSpec((1, D_MODEL), const),
        ],
        out_specs=pl.BlockSpec((1, tm, D_MODEL), row),
        out_shape=jax.ShapeDtypeStruct((b, s, D_MODEL), F32),
        compiler_params=_cparams(2),
        name="mix_prompt" if with_mem else "mix_sample",
    )(o_self, *mem_inputs, h, w_out, g_post)


def _ffn_kernel(h_ref, gpre_ref, gpost_ref, wgu_ref, wd_ref, out_ref):
    h = h_ref[0]
    xn = _rms(h, gpre_ref[...]).astype(BF16)
    f = jnp.zeros(h.shape, F32)
    for c in range(D_FF // FF_CHUNK):
        lo, hi = c * FF_CHUNK, (c + 1) * FF_CHUNK
        gate = _dot(xn, wgu_ref[:, lo:hi])
        up = _dot(xn, wgu_ref[:, D_FF + lo:D_FF + hi])
        a = (gate * jax.nn.sigmoid(gate) * up).astype(BF16)
        f = f + _dot(a, wd_ref[lo:hi, :])
    out_ref[0] = h + _rms(f, gpost_ref[...])


def _ffn(h, g_pre, g_post, w_gu, w_down, tm):
    b, s, _ = h.shape
    row = lambda i, j: (i, j, 0)
    const = lambda i, j: (0, 0)
    return pl.pallas_call(
        _ffn_kernel,
        grid=(b, s // tm),
        in_specs=[
            pl.BlockSpec((1, tm, D_MODEL), row),
            pl.BlockSpec((1, D_MODEL), const),
            pl.BlockSpec((1, D_MODEL), const),
            pl.BlockSpec((D_MODEL, 2 * D_FF), const, pipeline_mode=pl.Buffered(1)),
            pl.BlockSpec((D_FF, D_MODEL), const, pipeline_mode=pl.Buffered(1)),
        ],
        out_specs=pl.BlockSpec((1, tm, D_MODEL), row),
        out_shape=jax.ShapeDtypeStruct((b, s, D_MODEL), F32),
        compiler_params=_cparams(2),
        name="ffn",
    )(h, g_pre, g_post, w_gu, w_down)


def _decode_step(pt_ref, seq, c, q_ref, kcur_ref, vcur_ref, lfcur_ref, qm_ref, mk_ref, mv_ref,
                 lam_ref, g_ref, k_pages, v_pages, lf_ref, o_ref, om_ref, scratch, *, fox,
                 n_chunks, n_pages, lam_init, before_chunk, beside_chunk):
    pages = len(k_pages)
    chunk = pages * PAGE_SIZE
    if fox:
        m_scr, l_scr, acc_scr, carry_scr, lf_scr, p_scr, a_scr = scratch
    else:
        m_scr, l_scr, acc_scr = scratch

    @pl.when(c == 0)
    def _():
        m_scr[...] = jnp.full(m_scr.shape, NEG_INF, F32)
        l_scr[...] = jnp.zeros(l_scr.shape, F32)
        acc_scr[...] = jnp.zeros(acc_scr.shape, F32)
        if fox:
            carry_scr[...] = jnp.zeros(carry_scr.shape, F32)
            lf_scr[...] = jnp.zeros(lf_scr.shape, F32)

    before_chunk()
    sel = _head_selector(SELF_WIDTH)
    q_rows = jnp.where(sel, jnp.broadcast_to(q_ref[0], (HEAD_ROWS, SELF_WIDTH)), 0.0)
    q_blk = q_rows.astype(BF16)

    first_maybe_invalid = n_pages - (n_chunks - 1) * pages
    token = lax.broadcasted_iota(jnp.int32, (1, chunk), 1)
    valid_tokens = token < (n_pages - c * pages) * PAGE_SIZE

    if fox:
        k_cat = jnp.concatenate([kp[0].astype(BF16) for kp in k_pages], axis=1)
        s_all = _dot(q_blk, k_cat)
        lf_parts = []
        for i in range(pages):
            page = pt_ref[seq, jnp.minimum(c * pages + i, n_pages - 1)]
            for h in range(N_HEADS_FOX):
                lf_scr[i * HEAD_ROWS + h:i * HEAD_ROWS + h + 1, :] = lf_ref[h, pl.ds(page, 1), :]
            lf_i = lf_scr[i * HEAD_ROWS:(i + 1) * HEAD_ROWS, :]
            if i >= first_maybe_invalid:
                lf_i = jnp.where(c * pages + i < n_pages, lf_i, 0.0)
            lf_parts.append(lf_i)
        local = _dot_exact_right(jnp.concatenate(lf_parts, axis=0), _upper_tri(PAGE_SIZE))
        carry = carry_scr[:, 0:1]
        cums = []
        for i in range(pages):
            cum = local[i * HEAD_ROWS:(i + 1) * HEAD_ROWS, :] + carry
            carry = cum[:, PAGE_SIZE - 1:PAGE_SIZE]
            cums.append(cum)
        carry_scr[...] = jnp.broadcast_to(carry, carry_scr.shape)
        s_all = s_all - jnp.concatenate(cums, axis=1)
    else:
        s_all = None
        for pr in range(N_HEADS_DIFF // 2):
            k_pair = jnp.concatenate(
                [jnp.concatenate([kp[0, 2 * pr], kp[0, 2 * pr + 1]], axis=1).astype(BF16)
                 for kp in k_pages], axis=0)
            part = _dot_nt(q_blk[:, 2 * pr * LANES:2 * (pr + 1) * LANES], k_pair)
            s_all = part if s_all is None else s_all + part
    if first_maybe_invalid < pages:
        s_all = jnp.where(valid_tokens, s_all, NEG_INF)

    m_old = m_scr[:, 0:1]
    m_new = jnp.maximum(m_old, jnp.max(s_all, axis=1, keepdims=True))
    alpha = jnp.exp(m_old - m_new)
    p_all = jnp.exp(s_all - m_new)
    l_new = alpha * l_scr[:, 0:1] + jnp.sum(p_all, axis=1, keepdims=True)
    m_scr[...] = jnp.broadcast_to(m_new, m_scr.shape)
    l_scr[...] = jnp.broadcast_to(l_new, l_scr.shape)
    if fox:
        p_scr[...] = p_all
        a_scr[...] = jnp.broadcast_to(alpha, a_scr.shape)

        for h in range(N_HEADS_FOX):
            rows = slice(h * HEAD_DIM, (h + 1) * HEAD_DIM)
            acc_h = acc_scr[rows, :] * jnp.broadcast_to(a_scr[h:h + 1, :], (HEAD_DIM, LANES))
            for i in range(pages):
                p_row = p_scr[h:h + 1, i * PAGE_SIZE:(i + 1) * PAGE_SIZE]
                acc_h = acc_h + v_pages[i][0, rows, :] * jnp.broadcast_to(p_row, (HEAD_DIM, LANES))
            acc_scr[rows, :] = acc_h
    else:
        v_cat = jnp.concatenate(
            [jnp.concatenate([vp[0, h] for h in range(N_HEADS_DIFF)], axis=1).astype(BF16)
             for vp in v_pages], axis=0)
        acc_scr[...] = alpha * acc_scr[...] + _dot(p_all.astype(BF16), v_cat)
    beside_chunk()

    @pl.when(c == n_chunks - 1)
    def _():
        s_cur = jnp.sum(q_rows * kcur_ref[0], axis=1, keepdims=True)
        if fox:
            s_cur = s_cur - (carry + lfcur_ref[0])
        m_fin = jnp.maximum(m_new, s_cur)
        a_fin = jnp.exp(m_new - m_fin)
        p_cur = jnp.exp(s_cur - m_fin)
        l_fin = a_fin * l_new + p_cur
        if fox:
            spread = lambda col: jnp.sum(jnp.where(sel, col, 0.0), axis=0, keepdims=True)
            pv_row = jnp.sum(acc_scr[...].T, axis=0, keepdims=True)
            o_ref[0] = (spread(a_fin) * pv_row + spread(p_cur) * vcur_ref[0]) / spread(l_fin)
        else:
            o_rows = (a_fin * acc_scr[...] + p_cur * vcur_ref[0]) / l_fin
            lam = _diff_lambda(lam_ref, lam_init)
            r = lax.broadcasted_iota(jnp.int32, (HEAD_ROWS, SELF_WIDTH), 0)
            f = lax.broadcasted_iota(jnp.int32, (HEAD_ROWS, SELF_WIDTH), 1)
            head2 = 2 * (f // (2 * HEAD_DIM))
            coef = jnp.where(r == head2, 1.0, 0.0) - lam * jnp.where(r == head2 + 1, 1.0, 0.0)
            o = jnp.sum(coef * o_rows, axis=0, keepdims=True)
            for h in range(N_HEADS_DIFF):
                sl = slice(h * LANES, (h + 1) * LANES)
                o_ref[0, :, sl] = _head_norm(o[:, sl], g_ref[...], lam_init)

        sel_m = _head_selector(MEM_WIDTH)
        qm_blk = jnp.where(sel_m, jnp.broadcast_to(qm_ref[0], (HEAD_ROWS, MEM_WIDTH)), 0.0)
        s_m = _dot(qm_blk.astype(BF16), mk_ref[0, 0].astype(BF16))
        p_m = jnp.exp(s_m - jnp.max(s_m, axis=1, keepdims=True))
        l_m = jnp.sum(p_m, axis=1, keepdims=True)
        o_m = _dot_nt(p_m.astype(BF16), mv_ref[0, 0].astype(BF16)) / l_m
        om_ref[0] = jnp.sum(jnp.where(sel_m, o_m, 0.0), axis=0, keepdims=True)


def _attn_kernel(*refs, fox, tile, lam_init, pages, n_chunks, n_pages):
    it = iter(refs)
    pt_ref = next(it)
    q_ref, k_ref, v_ref = next(it), next(it), next(it)
    extra_refs = () if fox else (next(it), next(it))
    qs_ref, kcur_ref, vcur_ref = next(it), next(it), next(it)
    lfcur_ref = next(it) if fox else None
    qm_ref, mk_ref, mv_ref = next(it), next(it), next(it)
    k_pages = [next(it) for _ in range(pages)]
    v_pages = [next(it) for _ in range(pages)]
    lf_ref = next(it) if fox else None
    o_ref, os_ref, om_ref = next(it), next(it), next(it)
    scratch = list(it)

    pair, qi = pl.program_id(1), pl.program_id(2)
    step = (pl.program_id(0) * pl.num_programs(1) + pair) * pl.num_programs(2) + qi
    lam_ref, g_ref = (None, None) if fox else extra_refs
    diagonal = []

    def prompt_loop():
        if fox:
            diagonal.append(_prompt_attention_fox(q_ref, k_ref, v_ref, o_ref, tile=tile,
                                                  pair=pair, qi=qi))
        else:
            diagonal.append(_prompt_attention(q_ref, k_ref, v_ref, extra_refs, o_ref, tile=tile,
                                              lam_init=lam_init, qi=qi))

    _decode_step(pt_ref, step // n_chunks, step % n_chunks, qs_ref, kcur_ref, vcur_ref, lfcur_ref,
                 qm_ref, mk_ref, mv_ref, lam_ref, g_ref, k_pages, v_pages, lf_ref, os_ref, om_ref,
                 scratch, fox=fox, n_chunks=n_chunks, n_pages=n_pages, lam_init=lam_init,
                 before_chunk=prompt_loop, beside_chunk=lambda: diagonal[0]())


def _attn(q, k, v, extras, page_table, q_s, k_cur, v_cur, lf_cur, qm_s, mem_k, mem_v, layer,
          cache_k, cache_v, cache_lf, fox, tile, lam_init=0.0):
    b, n_blocks, s = (q.shape[0], q.shape[1], q.shape[3]) if fox else (
        q.shape[0], q.shape[2] // LANES, q.shape[1])
    w = n_blocks * LANES
    nq = s // tile
    db, n_pages = page_table.shape
    n_steps = b * n_blocks * nq
    assert n_steps % db == 0
    n_chunks = n_steps // db
    pages = -(-n_pages // n_chunks)

    def seq_of(i, p, j):
        return ((i * n_blocks + p) * nq + j) // n_chunks

    def page_of(i, p, j, pt, slot):
        step = (i * n_blocks + p) * nq + j
        return pt[step // n_chunks, jnp.minimum((step % n_chunks) * pages + slot, n_pages - 1)]

    qo_spec = pl.BlockSpec((1, tile, LANES), lambda i, p, j, pt: (i, j, p))
    if fox:
        qkv_specs = [pl.BlockSpec((1, 1, LANES, tile), lambda i, p, j, pt: (i, p, 0, j)),
                     pl.BlockSpec((1, 1, s, 2 * LANES), lambda i, p, j, pt: (i, p, 0, 0)),
                     pl.BlockSpec((1, 1, nq, LANES, tile), lambda i, p, j, pt: (i, p, 0, 0, 0))]
        extra_specs = []
    else:
        kv_spec = pl.BlockSpec((1, 1, s, LANES), lambda i, p, j, pt: (i, p, 0, 0))
        qkv_specs = [qo_spec, kv_spec, kv_spec]
        extra_specs = [pl.BlockSpec((8, LANES), lambda i, p, j, pt: (0, 0)),
                       pl.BlockSpec((1, LANES), lambda i, p, j, pt: (0, 0))]
    inputs = [q, k, v, *extras]
    in_specs = qkv_specs + extra_specs

    seq3 = lambda i, p, j, pt: (seq_of(i, p, j), 0, 0)
    inputs += [q_s, k_cur, v_cur]
    in_specs += [pl.BlockSpec((1, 1, SELF_WIDTH), seq3)] * 3
    if fox:
        inputs.append(lf_cur)
        in_specs.append(pl.BlockSpec((1, HEAD_ROWS, 1), seq3))
    mem_spec = pl.BlockSpec((1, 1, MEM_WIDTH, N_MEM),
                            lambda i, p, j, pt: (layer, seq_of(i, p, j), 0, 0))
    inputs += [qm_s, mem_k, mem_v]
    in_specs += [pl.BlockSpec((1, 1, MEM_WIDTH), seq3), mem_spec, mem_spec]
    if fox:
        page_specs = [pl.BlockSpec((1, SELF_WIDTH, PAGE_SIZE),
                                   lambda i, p, j, pt, t=t: (page_of(i, p, j, pt, t), 0, 0))
                      for t in range(pages)]
    else:
        page_specs = [pl.BlockSpec((1, N_HEADS_DIFF, PAGE_SIZE, LANES),
                                   lambda i, p, j, pt, t=t: (page_of(i, p, j, pt, t), 0, 0, 0))
                      for t in range(pages)]
    inputs += [cache_k] * pages + [cache_v] * pages
    in_specs += page_specs * 2
    if fox:
        inputs.append(cache_lf)
        in_specs.append(pl.BlockSpec(cache_lf.shape, lambda i, p, j, pt: (0, 0, 0),
                                     pipeline_mode=pl.Buffered(1)))
    stat = pltpu.VMEM((HEAD_ROWS, LANES), F32)
    if fox:
        scratch = [stat, stat, pltpu.VMEM((SELF_WIDTH, LANES), F32),
                   stat, pltpu.VMEM((pages * HEAD_ROWS, PAGE_SIZE), F32),
                   pltpu.VMEM((HEAD_ROWS, pages * PAGE_SIZE), F32), stat]
    else:
        scratch = [stat, stat, pltpu.VMEM((HEAD_ROWS, SELF_WIDTH), F32)]
    grid_spec = pltpu.PrefetchScalarGridSpec(
        num_scalar_prefetch=1,
        grid=(b, n_blocks, nq),
        in_specs=in_specs,
        out_specs=(qo_spec, pl.BlockSpec((1, 1, SELF_WIDTH), seq3),
                   pl.BlockSpec((1, 1, MEM_WIDTH), seq3)),
        scratch_shapes=scratch,
    )
    return pl.pallas_call(
        functools.partial(_attn_kernel, fox=fox, tile=tile, lam_init=lam_init, pages=pages,
                          n_chunks=n_chunks, n_pages=n_pages),
        grid_spec=grid_spec,
        out_shape=(jax.ShapeDtypeStruct((b, s, w), BF16),
                   jax.ShapeDtypeStruct((db, 1, SELF_WIDTH), F32),
                   jax.ShapeDtypeStruct((db, 1, MEM_WIDTH), F32)),
        compiler_params=_cparams(3),
        name="attn_fox" if fox else "attn_diff",
    )(page_table, *inputs)


def _rope_tables(pos):
    inv = ROPE_THETA ** (-jnp.arange(0, HEAD_DIM, 2, dtype=F32) / HEAD_DIM)
    ang = pos.astype(F32)[:, None] * inv[None, :]
    cos = jnp.cos(ang)
    sin = jnp.sin(ang)
    return jnp.tile(cos, (1, 4)), jnp.tile(jnp.concatenate([-sin, sin], axis=-1), (1, 2))


def kernel(x_prompt, x_sample, cache_fox_k, cache_fox_v, cache_fox_logf, cache_diff_k, cache_diff_v, cache_mem_k, cache_mem_v, page_table, mem_prompt, w_in_fox, b_f_fox, w_in_diff, lam_q1, lam_k1, lam_q2, lam_k2, g_subln, g_pre_mix, g_post_mix, g_pre_ffn, g_post_ffn, g_mem, w_mem_kv, w_out, w_gate_up, w_down):
    depth = g_pre_mix.shape[0]
    batch, seq, _ = x_prompt.shape
    dec_batch, dec_seq, _ = x_sample.shape
    assert dec_seq == 1
    n_pool = cache_fox_k.shape[1]
    past_len = page_table.shape[1] * PAGE_SIZE
    w = SELF_WIDTH
    tm = PROMPT_TILE

    h_p = x_prompt
    h_s = x_sample.reshape(dec_batch, D_MODEL)

    mem_kt, mem_vt, mem_ktb, mem_vtb = _mem_kv(
        mem_prompt, g_mem, jnp.swapaxes(w_mem_kv, 1, 2).astype(BF16))
    mem_cache_kt = jnp.transpose(cache_mem_k, (0, 1, 3, 4, 2)).reshape(
        depth, dec_batch, MEM_WIDTH, N_MEM)
    mem_cache_vt = jnp.transpose(cache_mem_v, (0, 1, 3, 4, 2)).reshape(
        depth, dec_batch, MEM_WIDTH, N_MEM)

    fox_kt = jnp.transpose(cache_fox_k, (0, 1, 3, 4, 2)).reshape(-1, w, PAGE_SIZE)
    fox_vt = jnp.transpose(cache_fox_v, (0, 1, 3, 4, 2)).reshape(-1, w, PAGE_SIZE)
    fox_lft = jnp.transpose(cache_fox_logf, (0, 3, 1, 2))
    diff_k = jnp.transpose(cache_diff_k, (0, 1, 3, 2, 4)).reshape(-1, N_HEADS_DIFF, PAGE_SIZE, LANES)
    diff_v = jnp.transpose(cache_diff_v, (0, 1, 3, 2, 4)).reshape(-1, N_HEADS_DIFF, PAGE_SIZE, LANES)

    cos_p, sin_p = _rope_tables(jnp.arange(seq))
    cos_s, sin_s = _rope_tables(past_len + jnp.zeros((dec_batch,), jnp.int32))

    outs = {name: [] for name in ("fk_p", "fv_p", "fl_p", "fk_s", "fv_s", "fl_s",
                                  "dk_p", "dv_p", "dk_s", "dv_s")}
    for i in range(depth):
        j = i // 2
        g_pre = g_pre_mix[i].reshape(1, D_MODEL)
        w_out_b = w_out[i].astype(BF16)
        w_gu_b = w_gate_up[i].astype(BF16)
        w_down_b = w_down[i].astype(BF16)
        if i % 2 == 0:
            w_in = w_in_fox[j]
            w_in_t = w_in.T
            gate_lo, gate_hi = 3 * w, 3 * w + N_HEADS_FOX
            wgt = jnp.zeros((HEAD_ROWS, D_MODEL), F32).at[:N_HEADS_FOX].set(
                w_in_t[gate_lo:gate_hi]).astype(BF16)
            bf_col = jnp.zeros((HEAD_ROWS, 1), F32).at[:N_HEADS_FOX, 0].set(b_f_fox[j])
            w_nat = jnp.concatenate([w_in[:, :gate_lo], w_in[:, gate_hi:]], axis=1).astype(BF16)
            wkgt = jnp.concatenate([w_in_t[w:2 * w].astype(BF16), wgt], axis=0)
            qt, kaug, kt, vt, vtb, qm, lft = _proj_fox_prompt(
                h_p, g_pre, w_in_t[:w].astype(BF16), w_nat[:, w:2 * w], wkgt,
                w_in_t[2 * w:3 * w].astype(BF16), w_nat[:, 3 * w:], bf_col, tm)
            to_tokens = lambda t: jnp.transpose(
                t.reshape(batch, N_HEADS_FOX, HEAD_DIM, seq), (0, 3, 1, 2))
            outs["fk_p"].append(to_tokens(kt))
            outs["fv_p"].append(to_tokens(vt))
            outs["fl_p"].append(jnp.swapaxes(lft[:, :N_HEADS_FOX], 1, 2))

            q_s, k, v, qm_s, lft_s = _proj_sample(h_s, g_pre, w_nat, (wgt, bf_col), fox=True)
            o_p, o_s, om_s = _attn(
                qt, kaug, vtb, (), page_table,
                q_s.reshape(dec_batch, 1, w), k.reshape(dec_batch, 1, w),
                v.reshape(dec_batch, 1, w), lft_s.T.reshape(dec_batch, HEAD_ROWS, 1),
                qm_s.reshape(dec_batch, 1, MEM_WIDTH), mem_cache_kt, mem_cache_vt, i,
                fox_kt[j * n_pool:(j + 1) * n_pool], fox_vt[j * n_pool:(j + 1) * n_pool],
                fox_lft[j], fox=True, tile=tm)
            outs["fk_s"].append(k.reshape(dec_batch, 1, N_HEADS_FOX, HEAD_DIM))
            outs["fv_s"].append(v.reshape(dec_batch, 1, N_HEADS_FOX, HEAD_DIM))
            outs["fl_s"].append(lft_s[:N_HEADS_FOX].T.reshape(dec_batch, 1, N_HEADS_FOX))
        else:
            lam_init = 0.8 - 0.6 * math.exp(-0.3 * i)
            w_all = w_in_diff[j].astype(BF16)
            lam_vec = jnp.zeros((8, LANES), F32)
            lam_vec = lam_vec.at[0, :HEAD_DIM].set(lam_q1[j]).at[1, :HEAD_DIM].set(lam_k1[j])
            lam_vec = lam_vec.at[2, :HEAD_DIM].set(lam_q2[j]).at[3, :HEAD_DIM].set(lam_k2[j])
            g_sub = g_subln[j].reshape(1, LANES)
            q, k, v, kb, vb, qm = _proj_diff_prompt(h_p, g_pre, w_all, cos_p, sin_p, tm)
            outs["dk_p"].append(jnp.swapaxes(k, 1, 2))
            outs["dv_p"].append(jnp.swapaxes(v, 1, 2))

            q_s, k, v, qm_s = _proj_sample(h_s, g_pre, w_all, (cos_s, sin_s), fox=False)
            o_p, o_s, om_s = _attn(
                q, kb, vb, (lam_vec, g_sub), page_table,
                q_s.reshape(dec_batch, 1, w), k.reshape(dec_batch, 1, w),
                v.reshape(dec_batch, 1, w), None,
                qm_s.reshape(dec_batch, 1, MEM_WIDTH), mem_cache_kt, mem_cache_vt, i,
                diff_k[j * n_pool:(j + 1) * n_pool], diff_v[j * n_pool:(j + 1) * n_pool],
                None, fox=False, tile=tm, lam_init=lam_init)
            outs["dk_s"].append(k.reshape(dec_batch, 1, N_HEADS_DIFF, 2 * HEAD_DIM))
            outs["dv_s"].append(v.reshape(dec_batch, 1, N_HEADS_DIFF, 2 * HEAD_DIM))

        g_post = g_post_mix[i].reshape(1, D_MODEL)
        g_pre_f = g_pre_ffn[i].reshape(1, D_MODEL)
        g_post_f = g_post_ffn[i].reshape(1, D_MODEL)
        h_p = _mix(o_p, (qm, mem_ktb, mem_vtb), h_p, w_out_b, g_post, tm, True, layer=i)
        h_p = _ffn(h_p, g_pre_f, g_post_f, w_gu_b, w_down_b, tm)
        h_s3 = _mix(o_s.reshape(1, dec_batch, w), (om_s.reshape(1, dec_batch, MEM_WIDTH),),
                    h_s.reshape(1, dec_batch, D_MODEL), w_out_b, g_post, dec_batch, False)
        h_s = _ffn(h_s3, g_pre_f, g_post_f, w_gu_b, w_down_b, dec_batch).reshape(
            dec_batch, D_MODEL)

    st = lambda name: jnp.stack(outs[name])
    mem_tokens = lambda t: jnp.transpose(
        t.reshape(depth, batch, MEM_WIDTH // HEAD_DIM, HEAD_DIM, N_MEM), (0, 1, 4, 2, 3))
    return (h_p, h_s.reshape(dec_batch, 1, D_MODEL), st("fk_p"), st("fv_p"), st("fl_p"),
            st("fk_s"), st("fv_s"), st("fl_s"), st("dk_p"), st("dv_p"), st("dk_s"), st("dv_s"),
            mem_tokens(mem_kt), mem_tokens(mem_vt))
```
